```python
import math
import jax, jax.numpy as jnp
from jax import lax
import numpy as np

D_MODEL = 2048
BATCH = 4
SEQ = 8192
DEPTH = 2

N_A_LAYERS = max(1, DEPTH // 2)
N_B_LAYERS = DEPTH - N_A_LAYERS

SB_HEAD_DIM = 128
SB_HEADS = D_MODEL // SB_HEAD_DIM

DIFF_HEAD_DIM = 128
DIFF_HEADS = D_MODEL // (2 * DIFF_HEAD_DIM)
DIFF_V_DIM = 2 * DIFF_HEAD_DIM
DIFF_QK_WIDTH = 2 * DIFF_HEADS * DIFF_HEAD_DIM
DIFF_V_WIDTH = DIFF_HEADS * DIFF_V_DIM

ROT_DIM = DIFF_HEAD_DIM // 4
ROPE_THETA = 500000.0

D_FF = 4 * D_MODEL

DEEPNORM_ALPHA = (2 * DEPTH) ** 0.25
DEEPNORM_BETA = (8 * DEPTH) ** -0.25

BLOCK_Q = 128
LN_EPS = 1e-5
SUBLN_EPS = 1e-5

kernel_name = "yoco_stickbreak_diffattn_hybrid"


def lambda_init_for(layer):
    return 0.8 - 0.6 * math.exp(-0.3 * layer)


def layer_norm(x, g, b):
    xf = x.astype(jnp.float32)
    mu = jnp.mean(xf, axis=-1, keepdims=True)
    var = jnp.mean(jnp.square(xf - mu), axis=-1, keepdims=True)
    y = (xf - mu) * lax.rsqrt(var + LN_EPS) * g.astype(jnp.float32) + b.astype(jnp.float32)
    return y.astype(x.dtype)


def squared_relu_mlp(x, w_up, w_down):
    h = jax.nn.relu(x @ w_up)
    return (h * h) @ w_down


def rotary_tables(seq_len):
    inv_freq = ROPE_THETA ** (-jnp.arange(0, ROT_DIM, 2, dtype=jnp.float32) / ROT_DIM)
    ang = jnp.arange(seq_len, dtype=jnp.float32)[:, None] * inv_freq[None, :]
    return jnp.cos(ang), jnp.sin(ang)


def partial_rotary(t, cos, sin):
    half = ROT_DIM // 2
    r1 = t[..., :half].astype(jnp.float32)
    r2 = t[..., half:ROT_DIM].astype(jnp.float32)
    rotated = jnp.concatenate([r1 * cos - r2 * sin, r2 * cos + r1 * sin], axis=-1).astype(t.dtype)
    return jnp.concatenate([rotated, t[..., ROT_DIM:]], axis=-1)


def merge_blocks(o):
    n_blk, b, h, bq, dv = o.shape
    return o.transpose(1, 0, 3, 2, 4).reshape(b, n_blk * bq, h * dv)


def stick_breaking_attention(x, w_qkv, w_o):
    b, s, _ = x.shape
    n_blk = s // BLOCK_Q
    qkv = (x @ w_qkv).reshape(b, s, 3, SB_HEADS, SB_HEAD_DIM).transpose(2, 0, 3, 1, 4)
    q, k, v = qkv[0], qkv[1], qkv[2]
    q_blocks = q.reshape(b, SB_HEADS, n_blk, BLOCK_Q, SB_HEAD_DIM).transpose(2, 0, 1, 3, 4)
    scale = SB_HEAD_DIM ** -0.5
    k_pos = jnp.arange(s)

    def block(args):
        qb, blk = args
        q_pos = blk * BLOCK_Q + jnp.arange(BLOCK_Q)
        z = jnp.einsum('bhqd,bhkd->bhqk', qb, k, preferred_element_type=jnp.float32) * scale
        strict = k_pos[None, :] < q_pos[:, None]
        neg_log_keep = jnp.where(strict, jax.nn.softplus(z), 0.0)
        shifted = jnp.concatenate([neg_log_keep[..., 1:], jnp.zeros_like(neg_log_keep[..., :1])], axis=-1)
        later = lax.cumsum(shifted, axis=3, reverse=True)
        log_a = jax.nn.log_sigmoid(z) - later
        a = jnp.where(strict, jnp.exp(log_a), 0.0)
        return jnp.einsum('bhqk,bhkd->bhqd', a.astype(v.dtype), v)

    o = lax.map(block, (q_blocks, jnp.arange(n_blk)))
    return merge_blocks(o) @ w_o


def shared_kv(x, w_kv):
    b, s, _ = x.shape
    kv = x @ w_kv
    k = kv[..., :DIFF_QK_WIDTH].reshape(b, s, DIFF_HEADS, 2, DIFF_HEAD_DIM).transpose(0, 2, 3, 1, 4)
    v = kv[..., DIFF_QK_WIDTH:].reshape(b, s, DIFF_HEADS, DIFF_V_DIM).transpose(0, 2, 1, 3)
    cos, sin = rotary_tables(s)
    return partial_rotary(k, cos, sin), v


def differential_attention(x, k, v, w_q, lam, subln_g, w_o, lambda_init):
    b, s, _ = x.shape
    n_blk = s // BLOCK_Q
    cos, sin = rotary_tables(s)
    q = (x @ w_q).reshape(b, s, DIFF_HEADS, 2, DIFF_HEAD_DIM).transpose(0, 2, 3, 1, 4)
    q = partial_rotary(q, cos, sin)
    q_blocks = q.reshape(b, DIFF_HEADS, 2, n_blk, BLOCK_Q, DIFF_HEAD_DIM).transpose(3, 0, 1, 2, 4, 5)
    lam_f = lam.astype(jnp.float32)
    lam_full = (jnp.exp(jnp.sum(lam_f[0] * lam_f[1])) - jnp.exp(jnp.sum(lam_f[2] * lam_f[3]))
                + lambda_init)
    g = subln_g.astype(jnp.float32) * (1.0 - lambda_init)
    scale = DIFF_HEAD_DIM ** -0.5
    k_pos = jnp.arange(s)

    def block(args):
        qb, blk = args
        q_pos = blk * BLOCK_Q + jnp.arange(BLOCK_Q)
        sc = jnp.einsum('bhcqd,bhckd->bhcqk', qb, k, preferred_element_type=jnp.float32) * scale
        causal = k_pos[None, :] <= q_pos[:, None]
        p = jax.nn.softmax(jnp.where(causal, sc, -jnp.inf), axis=-1)
        w = p[:, :, 0] - lam_full * p[:, :, 1]
        o = jnp.einsum('bhqk,bhkd->bhqd', w.astype(v.dtype), v).astype(jnp.float32)
        o = o * lax.rsqrt(jnp.mean(o * o, axis=-1, keepdims=True) + SUBLN_EPS) * g
        return o.astype(x.dtype)

    o = lax.map(block, (q_blocks, jnp.arange(n_blk)))
    return merge_blocks(o) @ w_o


def setup_inputs(seed: int = 0) -> dict:
    key = jax.random.key(seed)
    ks = jax.random.split(key, 13)
    f32 = jnp.float32
    x = jax.random.normal(ks[0], (BATCH, SEQ, D_MODEL), f32)
    ln_g = 1.0 + 0.02 * jax.random.normal(ks[1], (DEPTH, 2, D_MODEL), f32)
    ln_b = 0.02 * jax.random.normal(ks[2], (DEPTH, 2, D_MODEL), f32)
    sb_w_qkv = jax.random.normal(ks[3], (N_A_LAYERS, D_MODEL, 3 * D_MODEL), f32) * D_MODEL ** -0.5
    sb_w_o = jax.random.normal(ks[4], (N_A_LAYERS, D_MODEL, D_MODEL), f32) * (D_MODEL ** -0.5 * DEEPNORM_BETA)
    kv_w = jax.random.normal(ks[5], (D_MODEL, DIFF_QK_WIDTH + DIFF_V_WIDTH), f32) * D_MODEL ** -0.5
    diff_w_q = jax.random.normal(ks[6], (N_B_LAYERS, D_MODEL, DIFF_QK_WIDTH), f32) * D_MODEL ** -0.5
    diff_lambda = 0.1 * jax.random.normal(ks[7], (N_B_LAYERS, 4, DIFF_HEAD_DIM), f32)
    diff_subln_g = 1.0 + 0.02 * jax.random.normal(ks[8], (N_B_LAYERS, DIFF_V_DIM), f32)
    diff_w_o = jax.random.normal(ks[9], (N_B_LAYERS, DIFF_V_WIDTH, D_MODEL), f32) * (DIFF_V_WIDTH ** -0.5 * DEEPNORM_BETA)
    mlp_w_up = jax.random.normal(ks[10], (DEPTH, D_MODEL, D_FF), f32) * D_MODEL ** -0.5
    mlp_w_down = jax.random.normal(ks[11], (DEPTH, D_FF, D_MODEL), f32) * (D_FF ** -0.5 * DEEPNORM_BETA)
    return {"x": x, "ln_g": ln_g, "ln_b": ln_b, "sb_w_qkv": sb_w_qkv, "sb_w_o": sb_w_o,
            "kv_w": kv_w, "diff_w_q": diff_w_q, "diff_lambda": diff_lambda,
            "diff_subln_g": diff_subln_g, "diff_w_o": diff_w_o,
            "mlp_w_up": mlp_w_up, "mlp_w_down": mlp_w_down}


def reference(x, ln_g, ln_b, sb_w_qkv, sb_w_o, kv_w, diff_w_q, diff_lambda, diff_subln_g,
              diff_w_o, mlp_w_up, mlp_w_down):
    k_shared = None
    v_shared = None
    for layer in range(DEPTH):
        if layer < N_A_LAYERS:
            h = stick_breaking_attention(x, sb_w_qkv[layer], sb_w_o[layer])
        else:
            if layer == N_A_LAYERS:
                k_shared, v_shared = shared_kv(x, kv_w)
            j = layer - N_A_LAYERS
            h = differential_attention(x, k_shared, v_shared, diff_w_q[j], diff_lambda[j],
                                       diff_subln_g[j], diff_w_o[j], lambda_init_for(layer))
        x = layer_norm(DEEPNORM_ALPHA * x + h, ln_g[layer, 0], ln_b[layer, 0])
        x = layer_norm(DEEPNORM_ALPHA * x + squared_relu_mlp(x, mlp_w_up[layer], mlp_w_down[layer]),
                       ln_g[layer, 1], ln_b[layer, 1])
    return x
```

```python
import functools
import math

import jax
import jax.numpy as jnp
from jax import lax
from jax.experimental import pallas as pl
from jax.experimental.pallas import tpu as pltpu

HEAD_DIM = 128
DIFF_V_DIM = 256
ROT_DIM = HEAD_DIM // 4
ROPE_THETA = 500000.0
LN_EPS = 1e-5
SUBLN_EPS = 1e-5

V7X_VMEM_LIMIT_BYTES = 56 * 1024 * 1024

F32 = jnp.float32
BF16 = jnp.bfloat16


def _nt_dot(a, b):
    return lax.dot_general(a, b, (((1,), (1,)), ((), ())), preferred_element_type=F32)


def _layer_norm_rows(y, g, b):
    mu = jnp.mean(y, axis=-1, keepdims=True)
    yc = y - mu
    var = jnp.mean(yc * yc, axis=-1, keepdims=True)
    return yc * lax.rsqrt(var + LN_EPS) * g + b


def _proj_kernel(x_ref, w_ref, c_ref, sa_ref, sb_ref, o_ref, *, rope_col_blocks):
    acc = jnp.dot(x_ref[...].astype(BF16), w_ref[...], preferred_element_type=F32)
    tn = acc.shape[1]
    if rope_col_blocks == 0:
        o_ref[...] = acc.astype(o_ref.dtype)
        return
    j = pl.program_id(1)

    @pl.when(j < rope_col_blocks)
    def _():
        c = c_ref[...]
        sa = sa_ref[...]
        sb = sb_ref[...]
        for g in range(tn // HEAD_DIM):
            t = acc[:, g * HEAD_DIM:(g + 1) * HEAD_DIM]
            rot = (t * c + pltpu.roll(t, HEAD_DIM - ROT_DIM // 2, 1) * sa
                   + pltpu.roll(t, ROT_DIM // 2, 1) * sb)
            o_ref[:, g * HEAD_DIM:(g + 1) * HEAD_DIM] = rot.astype(o_ref.dtype)

    @pl.when(j >= rope_col_blocks)
    def _():
        o_ref[...] = acc.astype(o_ref.dtype)


def _project(x2d, w, rope_tabs, rope_cols, seq, *, tm=1024, tn=1024):
    m, k = x2d.shape
    n = w.shape[1]
    assert m % tm == 0 and n % tn == 0 and seq % tm == 0 and rope_cols % tn == 0
    row_blocks_per_seq = seq // tm
    tab_spec = pl.BlockSpec((tm, HEAD_DIM), lambda i, j: (i % row_blocks_per_seq, 0))
    return pl.pallas_call(
        functools.partial(_proj_kernel, rope_col_blocks=rope_cols // tn),
        out_shape=jax.ShapeDtypeStruct((m, n), BF16),
        grid=(m // tm, n // tn),
        in_specs=[
            pl.BlockSpec((tm, k), lambda i, j: (i, 0)),
            pl.BlockSpec((k, tn), lambda i, j: (0, j)),
            tab_spec, tab_spec, tab_spec,
        ],
        out_specs=pl.BlockSpec((tm, tn), lambda i, j: (i, j)),
        compiler_params=pltpu.CompilerParams(
            dimension_semantics=("parallel", "parallel"),
            vmem_limit_bytes=V7X_VMEM_LIMIT_BYTES),
        name="proj",
    )(x2d, w, *rope_tabs)


def _sb_tile(q, k, v, u, carry, scale, strict):
    z = _nt_dot(q, k) * scale
    l = jnp.log(1.0 + jnp.exp(-jnp.abs(z)))
    sp = jnp.maximum(z, 0.0) + l
    ls = jnp.minimum(z, 0.0) - l
    if strict is not None:
        sp = jnp.where(strict, sp, 0.0)
    excl = jnp.dot(sp.astype(BF16), u, preferred_element_type=F32)
    a = jnp.exp(ls - carry - excl)
    if strict is not None:
        a = jnp.where(strict, a, 0.0)
    pv = jnp.dot(a.astype(BF16), v, preferred_element_type=F32)
    return pv, carry + jnp.sum(sp, axis=1, keepdims=True)


def _sb_kernel(q_ref, k_ref, v_ref, u_ref, o_ref, *, scale, blk):
    qi = pl.program_id(2)
    q = q_ref[...]
    u = u_ref[...]
    row = lax.broadcasted_iota(jnp.int32, (blk, blk), 0)
    col = lax.broadcasted_iota(jnp.int32, (blk, blk), 1)
    start = pl.multiple_of(qi * blk, blk)
    acc, carry = _sb_tile(q, k_ref[pl.ds(start, blk), :], v_ref[pl.ds(start, blk), :], u,
                          jnp.zeros((blk, 1), F32), scale, col < row)

    def body(i, state):
        acc, carry = state
        s = pl.multiple_of((qi - 1 - i) * blk, blk)
        pv, carry = _sb_tile(q, k_ref[pl.ds(s, blk), :], v_ref[pl.ds(s, blk), :], u,
                             carry, scale, None)
        return acc + pv, carry

    acc, _ = lax.fori_loop(0, qi, body, (acc, carry))
    o_ref[...] = acc.astype(o_ref.dtype)


def _stick_breaking_attention(qkv, batch, seq, heads, *, blk=256):
    m = qkv.shape[0]
    nq = seq // blk
    idx = jnp.arange(blk)
    u = (idx[:, None] > idx[None, :]).astype(BF16)
    return pl.pallas_call(
        functools.partial(_sb_kernel, scale=HEAD_DIM ** -0.5, blk=blk),
        out_shape=jax.ShapeDtypeStruct((m, heads * HEAD_DIM), BF16),
        grid=(batch, heads, nq),
        in_specs=[
            pl.BlockSpec((blk, HEAD_DIM), lambda b, h, i: (b * nq + i, h)),
            pl.BlockSpec((seq, HEAD_DIM), lambda b, h, i: (b, heads + h)),
            pl.BlockSpec((seq, HEAD_DIM), lambda b, h, i: (b, 2 * heads + h)),
            pl.BlockSpec((blk, blk), lambda b, h, i: (0, 0)),
        ],
        out_specs=pl.BlockSpec((blk, HEAD_DIM), lambda b, h, i: (b * nq + i, h)),
        compiler_params=pltpu.CompilerParams(
            dimension_semantics=("parallel", "parallel", "parallel"),
            vmem_limit_bytes=V7X_VMEM_LIMIT_BYTES),
        name="sb_attn",
    )(qkv, qkv, qkv, u)


def _diff_kernel(q_ref, k_ref, v_ref, lam_ref, g_ref, o_ref, *, scale, blk, lambda_init):
    qi = pl.program_id(2)
    q = q_ref[...]
    qs = (q[:, :HEAD_DIM], q[:, HEAD_DIM:])
    row = lax.broadcasted_iota(jnp.int32, (blk, blk), 0)
    col = lax.broadcasted_iota(jnp.int32, (blk, blk), 1)
    causal = col <= row

    def scores(c, k, mask):
        s = _nt_dot(qs[c], k[:, c * HEAD_DIM:(c + 1) * HEAD_DIM]) * scale
        return s if mask is None else jnp.where(mask, s, -jnp.inf)

    start = pl.multiple_of(qi * blk, blk)
    k0 = k_ref[pl.ds(start, blk), :]
    v0 = v_ref[pl.ds(start, blk), :]
    state = []
    for c in range(2):
        s = scores(c, k0, causal)
        m = jnp.max(s, axis=1, keepdims=True)
        p = jnp.exp(s - m)
        state += [m, jnp.sum(p, axis=1, keepdims=True),
                  jnp.dot(p.astype(BF16), v0, preferred_element_type=F32)]

    def body(i, state):
        s0 = pl.multiple_of(i * blk, blk)
        k = k_ref[pl.ds(s0, blk), :]
        v = v_ref[pl.ds(s0, blk), :]
        out = []
        for c in range(2):
            m, l, acc = state[3 * c:3 * c + 3]
            s = scores(c, k, None)
            m_new = jnp.maximum(m, jnp.max(s, axis=1, keepdims=True))
            alpha = jnp.exp(m - m_new)
            p = jnp.exp(s - m_new)
            out += [m_new, alpha * l + jnp.sum(p, axis=1, keepdims=True),
                    alpha * acc + jnp.dot(p.astype(BF16), v, preferred_element_type=F32)]
        return tuple(out)

    _, l1, acc1, _, l2, acc2 = lax.fori_loop(0, qi, body, tuple(state))

    lam = lam_ref[...]
    lam_full = (jnp.exp(jnp.sum(lam[0:1] * lam[1:2], axis=1, keepdims=True))
                - jnp.exp(jnp.sum(lam[2:3] * lam[3:4], axis=1, keepdims=True)) + lambda_init)
    o = acc1 / l1 - lam_full * (acc2 / l2)
    o = o * lax.rsqrt(jnp.mean(o * o, axis=-1, keepdims=True) + SUBLN_EPS)
    o_ref[...] = (o * (g_ref[...] * (1.0 - lambda_init))).astype(o_ref.dtype)


def _differential_attention(qkv, lam, subln_g, batch, seq, heads, lambda_init, *, blk=256):
    m = qkv.shape[0]
    nq = seq // blk
    w = DIFF_V_DIM
    return pl.pallas_call(
        functools.partial(_diff_kernel, scale=HEAD_DIM ** -0.5, blk=blk, lambda_init=lambda_init),
        out_shape=jax.ShapeDtypeStruct((m, heads * w), BF16),
        grid=(batch, heads, nq),
        in_specs=[
            pl.BlockSpec((blk, w), lambda b, h, i: (b * nq + i, h)),
            pl.BlockSpec((seq, w), lambda b, h, i: (b, heads + h)),
            pl.BlockSpec((seq, w), lambda b, h, i: (b, 2 * heads + h)),
            pl.BlockSpec((4, HEAD_DIM), lambda b, h, i: (0, 0)),
            pl.BlockSpec((1, w), lambda b, h, i: (0, 0)),
        ],
        out_specs=pl.BlockSpec((blk, w), lambda b, h, i: (b * nq + i, h)),
        compiler_params=pltpu.CompilerParams(
            dimension_semantics=("parallel", "parallel", "parallel"),
            vmem_limit_bytes=V7X_VMEM_LIMIT_BYTES),
        name="diff_attn",
    )(qkv, qkv, qkv, lam, subln_g.reshape(1, w))


def _out_ln_kernel(a_ref, w_ref, x_ref, g_ref, b_ref, of_ref, ob_ref, *, alpha):
    h = jnp.dot(a_ref[...], w_ref[...], preferred_element_type=F32)
    y = _layer_norm_rows(alpha * x_ref[...] + h, g_ref[...], b_ref[...])
    of_ref[...] = y
    ob_ref[...] = y.astype(BF16)


def _out_proj_ln(attn, w, x, g, b, alpha, *, tm=512):
    m, k = attn.shape
    d = w.shape[1]
    row = pl.BlockSpec((tm, d), lambda i: (i, 0))
    vec = pl.BlockSpec((1, d), lambda i: (0, 0))
    return pl.pallas_call(
        functools.partial(_out_ln_kernel, alpha=alpha),
        out_shape=(jax.ShapeDtypeStruct((m, d), F32), jax.ShapeDtypeStruct((m, d), BF16)),
        grid=(m // tm,),
        in_specs=[pl.BlockSpec((tm, k), lambda i: (i, 0)),
                  pl.BlockSpec((k, d), lambda i: (0, 0)),
                  row, vec, vec],
        out_specs=(row, row),
        compiler_params=pltpu.CompilerParams(
            dimension_semantics=("parallel",),
            vmem_limit_bytes=V7X_VMEM_LIMIT_BYTES),
        name="out_proj_ln",
    )(attn, w, x, g.reshape(1, d), b.reshape(1, d))


def _mlp_ln_kernel(xb_ref, wu_ref, wd_ref, x_ref, g_ref, b_ref, of_ref, ob_ref, acc_ref, *, alpha):
    j = pl.program_id(1)

    @pl.when(j == 0)
    def _():
        acc_ref[...] = jnp.zeros_like(acc_ref)

    h = jnp.maximum(jnp.dot(xb_ref[...], wu_ref[...], preferred_element_type=F32), 0.0)
    acc_ref[...] += jnp.dot((h * h).astype(BF16), wd_ref[...], preferred_element_type=F32)

    @pl.when(j == pl.num_programs(1) - 1)
    def _():
        y = _layer_norm_rows(alpha * x_ref[...] + acc_ref[...], g_ref[...], b_ref[...])
        of_ref[...] = y
        ob_ref[...] = y.astype(BF16)


def _mlp_ln(xb, w_up, w_down, x, g, b, alpha, *, tm=512, tf=512):
    m, d = xb.shape
    f = w_up.shape[1]
    row = lambda: pl.BlockSpec((tm, d), lambda i, j: (i, 0))
    vec = pl.BlockSpec((1, d), lambda i, j: (0, 0))
    return pl.pallas_call(
        functools.partial(_mlp_ln_kernel, alpha=alpha),
        out_shape=(jax.ShapeDtypeStruct((m, d), F32), jax.ShapeDtypeStruct((m, d), BF16)),
        grid=(m // tm, f // tf),
        in_specs=[row(),
                  pl.BlockSpec((d, tf), lambda i, j: (0, j)),
                  pl.BlockSpec((tf, d), lambda i, j: (j, 0)),
                  row(), vec, vec],
        out_specs=(row(), row()),
        scratch_shapes=[pltpu.VMEM((tm, d), F32)],
        compiler_params=pltpu.CompilerParams(
            dimension_semantics=("parallel", "arbitrary"),
            vmem_limit_bytes=V7X_VMEM_LIMIT_BYTES),
        name="mlp_ln",
    )(xb, w_up, w_down, x, g.reshape(1, d), b.reshape(1, d))


def _rope_tables(seq):
    half = ROT_DIM // 2
    inv_freq = ROPE_THETA ** (-jnp.arange(0, ROT_DIM, 2, dtype=F32) / ROT_DIM)
    ang = jnp.arange(seq, dtype=F32)[:, None] * inv_freq[None, :]
    cos, sin = jnp.cos(ang), jnp.sin(ang)
    zeros = lambda n: jnp.zeros((seq, n), F32)
    c = jnp.concatenate([cos, cos, jnp.ones((seq, HEAD_DIM - ROT_DIM), F32)], axis=1)
    sa = jnp.concatenate([-sin, zeros(HEAD_DIM - half)], axis=1)
    sb = jnp.concatenate([zeros(half), sin, zeros(HEAD_DIM - ROT_DIM)], axis=1)
    return c, sa, sb


def kernel(x, ln_g, ln_b, sb_w_qkv, sb_w_o, kv_w, diff_w_q, diff_lambda, diff_subln_g,
           diff_w_o, mlp_w_up, mlp_w_down):
    batch, seq, d_model = x.shape
    depth = ln_g.shape[0]
    n_a = sb_w_qkv.shape[0]
    alpha = (2 * depth) ** 0.25
    sb_heads = d_model // HEAD_DIM
    diff_heads = d_model // DIFF_V_DIM
    rope_tabs = _rope_tables(seq)

    xf = x.reshape(batch * seq, d_model)
    xb = xf
    kv_wb = kv_w.astype(BF16)
    for layer in range(depth):
        if layer < n_a:
            qkv = _project(xb, sb_w_qkv[layer].astype(BF16), rope_tabs, 0, seq)
            attn = _stick_breaking_attention(qkv, batch, seq, sb_heads)
            w_o = sb_w_o[layer]
        else:
            j = layer - n_a
            w_cat = jnp.concatenate([diff_w_q[j].astype(BF16), kv_wb], axis=1)
            qkv = _project(xb, w_cat, rope_tabs, 2 * d_model, seq)
            lambda_init = 0.8 - 0.6 * math.exp(-0.3 * layer)
            attn = _differential_attention(qkv, diff_lambda[j], diff_subln_g[j], batch, seq,
                                           diff_heads, lambda_init)
            w_o = diff_w_o[j]
        xf, xb = _out_proj_ln(attn, w_o.astype(BF16), xf, ln_g[layer, 0], ln_b[layer, 0], alpha)
        xf, xb = _mlp_ln(xb, mlp_w_up[layer].astype(BF16), mlp_w_down[layer].astype(BF16), xf,
                         ln_g[layer, 1], ln_b[layer, 1], alpha)
    return xf.reshape(batch, seq, d_model)
```

```python
import functools
import math

import jax
import jax.numpy as jnp
from jax import lax
from jax.experimental import pallas as pl
from jax.experimental.pallas import tpu as pltpu

HEAD_DIM = 128
DIFF_V_DIM = 256
ROT_DIM = HEAD_DIM // 4
ROPE_THETA = 500000.0
LN_EPS = 1e-5
SUBLN_EPS = 1e-5
LOGIT_SCALE_LOG2 = HEAD_DIM ** -0.5 * math.log2(math.e)
MASKED_LOGIT = -1e30

V7X_VMEM_LIMIT_BYTES = 56 * 1024 * 1024

F32 = jnp.float32
BF16 = jnp.bfloat16


def _nt_dot(a, b):
    return lax.dot_general(a, b, (((1,), (1,)), ((), ())), preferred_element_type=F32)


def _layer_norm_rows(y, g, b):
    mu = jnp.mean(y, axis=-1, keepdims=True)
    yc = y - mu
    var = jnp.mean(yc * yc, axis=-1, keepdims=True)
    return yc * lax.rsqrt(var + LN_EPS) * g + b


def _proj_kernel(x_ref, w_ref, c_ref, sa_ref, sb_ref, o_ref, *, rope_col_blocks, q_col_blocks,
                 q_scale):
    j = pl.program_id(1)
    acc = jnp.dot(x_ref[...].astype(BF16), w_ref[...], preferred_element_type=F32)
    acc = acc * jnp.where(j < q_col_blocks, q_scale, 1.0)
    tn = acc.shape[1]
    if rope_col_blocks == 0:
        o_ref[...] = acc.astype(o_ref.dtype)
        return

    @pl.when(j < rope_col_blocks)
    def _():
        c = c_ref[...]
        sa = sa_ref[...]
        sb = sb_ref[...]
        for g in range(tn // HEAD_DIM):
            t = acc[:, g * HEAD_DIM:(g + 1) * HEAD_DIM]
            rot = (t * c + pltpu.roll(t, HEAD_DIM - ROT_DIM // 2, 1) * sa
                   + pltpu.roll(t, ROT_DIM // 2, 1) * sb)
            o_ref[:, g * HEAD_DIM:(g + 1) * HEAD_DIM] = rot.astype(o_ref.dtype)

    @pl.when(j >= rope_col_blocks)
    def _():
        o_ref[...] = acc.astype(o_ref.dtype)


def _project(x2d, w, rope_tabs, rope_cols, q_cols, seq, *, tm=1024, tn=1024):
    m, k = x2d.shape
    n = w.shape[1]
    assert m % tm == 0 and n % tn == 0 and seq % tm == 0
    assert rope_cols % tn == 0 and q_cols % tn == 0
    row_blocks_per_seq = seq // tm
    tab_spec = pl.BlockSpec((tm, HEAD_DIM), lambda i, j: (i % row_blocks_per_seq, 0))
    return pl.pallas_call(
        functools.partial(_proj_kernel, rope_col_blocks=rope_cols // tn,
                          q_col_blocks=q_cols // tn, q_scale=LOGIT_SCALE_LOG2),
        out_shape=jax.ShapeDtypeStruct((m, n), BF16),
        grid=(m // tm, n // tn),
        in_specs=[
            pl.BlockSpec((tm, k), lambda i, j: (i, 0)),
            pl.BlockSpec((k, tn), lambda i, j: (0, j)),
            tab_spec, tab_spec, tab_spec,
        ],
        out_specs=pl.BlockSpec((tm, tn), lambda i, j: (i, j)),
        compiler_params=pltpu.CompilerParams(
            dimension_semantics=("parallel", "parallel"),
            vmem_limit_bytes=V7X_VMEM_LIMIT_BYTES),
        name="proj",
    )(x2d, w, *rope_tabs)


def _sb_tile(q, k, v, u, carry, strict):
    z = _nt_dot(q, k)
    neg_abs = lax.bitcast_convert_type(
        lax.bitcast_convert_type(z, jnp.uint32) | jnp.uint32(0x80000000), F32)
    sp = jnp.maximum(z, 0.0) + jnp.log2(1.0 + jnp.exp2(neg_abs))
    ls = z - sp
    if strict is not None:
        sp = jnp.where(strict, sp, 0.0)
    excl = jnp.dot(sp.astype(BF16), u, preferred_element_type=F32)
    a = jnp.exp2(ls - carry - excl)
    if strict is not None:
        a = jnp.where(strict, a, 0.0)
    pv = jnp.dot(a.astype(BF16), v, preferred_element_type=F32)
    return pv, carry + jnp.sum(sp, axis=1, keepdims=True)


def _sb_kernel(q_ref, k_ref, v_ref, u_ref, o_ref, *, blk, chunks):
    qi = pl.program_id(2)
    bq = blk * chunks
    u = u_ref[...]
    base = qi * chunks

    def kv_tile(t):
        s = pl.multiple_of(t * blk, blk)
        return k_ref[pl.ds(s, blk), :], v_ref[pl.ds(s, blk), :]

    acc = carry = None
    for t in reversed(range(chunks)):
        rows = bq - t * blk
        k, v = kv_tile(base + t)
        row = lax.broadcasted_iota(jnp.int32, (rows, blk), 0)
        col = lax.broadcasted_iota(jnp.int32, (rows, blk), 1)
        new_acc = jnp.zeros((blk, HEAD_DIM), F32)
        new_carry = jnp.zeros((blk, 1), F32)
        acc = new_acc if acc is None else jnp.concatenate([new_acc, acc], axis=0)
        carry = new_carry if carry is None else jnp.concatenate([new_carry, carry], axis=0)
        pv, carry = _sb_tile(q_ref[t * blk:, :], k, v, u, carry, col < row)
        acc = acc + pv

    def body(i, state):
        acc, carry = state
        k, v = kv_tile(base - 1 - i)
        pv, carry = _sb_tile(q_ref[...], k, v, u, carry, None)
        return acc + pv, carry

    acc, _ = lax.fori_loop(0, base, body, (acc, carry))
    o_ref[...] = acc.astype(o_ref.dtype)


def _stick_breaking_attention(qkv, batch, seq, heads, *, blk=256, chunks=4):
    m = qkv.shape[0]
    bq = blk * chunks
    nq = seq // bq
    idx = jnp.arange(blk)
    u = (idx[:, None] > idx[None, :]).astype(BF16)
    return pl.pallas_call(
        functools.partial(_sb_kernel, blk=blk, chunks=chunks),
        out_shape=jax.ShapeDtypeStruct((m, heads * HEAD_DIM), BF16),
        grid=(batch, heads, nq),
        in_specs=[
            pl.BlockSpec((bq, HEAD_DIM), lambda b, h, i: (b * nq + i, h)),
            pl.BlockSpec((seq, HEAD_DIM), lambda b, h, i: (b, heads + h)),
            pl.BlockSpec((seq, HEAD_DIM), lambda b, h, i: (b, 2 * heads + h)),
            pl.BlockSpec((blk, blk), lambda b, h, i: (0, 0)),
        ],
        out_specs=pl.BlockSpec((bq, HEAD_DIM), lambda b, h, i: (b * nq + i, h)),
        compiler_params=pltpu.CompilerParams(
            dimension_semantics=("parallel", "parallel", "parallel"),
            vmem_limit_bytes=V7X_VMEM_LIMIT_BYTES),
        name="sb_attn",
    )(qkv, qkv, qkv, u)


def _diff_kernel(q_ref, k_ref, v_ref, lam_ref, g_ref, o_ref, m_ref, l_ref, acc_ref, *,
                 blk, chunks, lambda_init):
    qi = pl.program_id(2)
    bq = blk * chunks
    base = qi * chunks
    groups = blk // HEAD_DIM

    def scores(c, r0, t, masked):
        lanes = slice(c * HEAD_DIM, (c + 1) * HEAD_DIM)
        s0 = pl.multiple_of(t * blk, blk)
        s = _nt_dot(q_ref[r0:, lanes], k_ref[pl.ds(s0, blk), lanes])
        if masked:
            row = lax.broadcasted_iota(jnp.int32, s.shape, 0)
            col = lax.broadcasted_iota(jnp.int32, s.shape, 1)
            s = jnp.where(col <= row, s, MASKED_LOGIT)
        return [s[:, g * HEAD_DIM:(g + 1) * HEAD_DIM] for g in range(groups)]

    def sweep(tile_fn):
        for t in range(chunks):
            tile_fn(t * blk, base + t, True)

        def body(i, carry):
            tile_fn(0, i, False)
            return carry

        lax.fori_loop(0, base, body, 0)

    def max_tile(r0, t, masked):
        for c in range(2):
            m_ref[c, r0:, :] = functools.reduce(
                jnp.maximum, scores(c, r0, t, masked), m_ref[c, r0:, :])

    def pv_tile(r0, t, masked):
        v = v_ref[pl.ds(pl.multiple_of(t * blk, blk), blk), :]
        for c in range(2):
            m = m_ref[c, r0:, :]
            p = [jnp.exp2(s - m) for s in scores(c, r0, t, masked)]
            l_ref[c, r0:, :] += functools.reduce(jnp.add, p)
            acc_ref[c, r0:, :] += jnp.dot(jnp.concatenate(p, axis=1).astype(BF16), v,
                                          preferred_element_type=F32)

    m_ref[...] = jnp.full(m_ref.shape, MASKED_LOGIT, F32)
    sweep(max_tile)
    for c in range(2):
        m_ref[c] = jnp.broadcast_to(jnp.max(m_ref[c], axis=1, keepdims=True), m_ref.shape[1:])
    l_ref[...] = jnp.zeros(l_ref.shape, F32)
    acc_ref[...] = jnp.zeros(acc_ref.shape, F32)
    sweep(pv_tile)

    lam = lam_ref[...]
    lam_full = (jnp.exp(jnp.sum(lam[0:1] * lam[1:2], axis=1, keepdims=True))
                - jnp.exp(jnp.sum(lam[2:3] * lam[3:4], axis=1, keepdims=True)) + lambda_init)
    inv_l = [1.0 / jnp.sum(l_ref[c], axis=1, keepdims=True) for c in range(2)]
    o = acc_ref[0] * inv_l[0] - lam_full * (acc_ref[1] * inv_l[1])
    o = o * lax.rsqrt(jnp.mean(o * o, axis=-1, keepdims=True) + SUBLN_EPS)
    o_ref[...] = (o * (g_ref[...] * (1.0 - lambda_init))).astype(o_ref.dtype)


def _differential_attention(qkv, lam, subln_g, batch, seq, heads, lambda_init, *, blk=256,
                            chunks=4):
    m = qkv.shape[0]
    bq = blk * chunks
    nq = seq // bq
    w = DIFF_V_DIM
    return pl.pallas_call(
        functools.partial(_diff_kernel, blk=blk, chunks=chunks, lambda_init=lambda_init),
        out_shape=jax.ShapeDtypeStruct((m, heads * w), BF16),
        grid=(batch, heads, nq),
        in_specs=[
            pl.BlockSpec((bq, w), lambda b, h, i: (b * nq + i, h)),
            pl.BlockSpec((seq, w), lambda b, h, i: (b, heads + h)),
            pl.BlockSpec((seq, w), lambda b, h, i: (b, 2 * heads + h)),
            pl.BlockSpec((4, HEAD_DIM), lambda b, h, i: (0, 0)),
            pl.BlockSpec((1, w), lambda b, h, i: (0, 0)),
        ],
        out_specs=pl.BlockSpec((bq, w), lambda b, h, i: (b * nq + i, h)),
        scratch_shapes=[pltpu.VMEM((2, bq, HEAD_DIM), F32), pltpu.VMEM((2, bq, HEAD_DIM), F32),
                        pltpu.VMEM((2, bq, w), F32)],
        compiler_params=pltpu.CompilerParams(
            dimension_semantics=("parallel", "parallel", "parallel"),
            vmem_limit_bytes=V7X_VMEM_LIMIT_BYTES),
        name="diff_attn",
    )(qkv, qkv, qkv, lam, subln_g.reshape(1, w))


def _out_ln_kernel(a_ref, w_ref, x_ref, g_ref, b_ref, of_ref, ob_ref, *, alpha):
    h = jnp.dot(a_ref[...], w_ref[...], preferred_element_type=F32)
    y = _layer_norm_rows(alpha * x_ref[...] + h, g_ref[...], b_ref[...])
    of_ref[...] = y
    ob_ref[...] = y.astype(BF16)


def _out_proj_ln(attn, w, x, g, b, alpha, *, tm=512):
    m, k = attn.shape
    d = w.shape[1]
    row = pl.BlockSpec((tm, d), lambda i: (i, 0))
    vec = pl.BlockSpec((1, d), lambda i: (0, 0))
    return pl.pallas_call(
        functools.partial(_out_ln_kernel, alpha=alpha),
        out_shape=(jax.ShapeDtypeStruct((m, d), F32), jax.ShapeDtypeStruct((m, d), BF16)),
        grid=(m // tm,),
        in_specs=[pl.BlockSpec((tm, k), lambda i: (i, 0)),
                  pl.BlockSpec((k, d), lambda i: (0, 0)),
                  row, vec, vec],
        out_specs=(row, row),
        compiler_params=pltpu.CompilerParams(
            dimension_semantics=("parallel",),
            vmem_limit_bytes=V7X_VMEM_LIMIT_BYTES),
        name="out_proj_ln",
    )(attn, w, x, g.reshape(1, d), b.reshape(1, d))


def _mlp_ln_kernel(xb_ref, wu_ref, wd_ref, x_ref, g_ref, b_ref, of_ref, ob_ref, acc_ref, *, alpha):
    j = pl.program_id(1)

    @pl.when(j == 0)
    def _():
        acc_ref[...] = jnp.zeros_like(acc_ref)

    h = jnp.maximum(jnp.dot(xb_ref[...], wu_ref[...], preferred_element_type=F32), 0.0)
    acc_ref[...] += jnp.dot((h * h).astype(BF16), wd_ref[...], preferred_element_type=F32)

    @pl.when(j == pl.num_programs(1) - 1)
    def _():
        y = _layer_norm_rows(alpha * x_ref[...] + acc_ref[...], g_ref[...], b_ref[...])
        of_ref[...] = y
        ob_ref[...] = y.astype(BF16)


def _mlp_ln(xb, w_up, w_down, x, g, b, alpha, *, tm=512, tf=512):
    m, d = xb.shape
    f = w_up.shape[1]
    row = lambda: pl.BlockSpec((tm, d), lambda i, j: (i, 0))
    vec = pl.BlockSpec((1, d), lambda i, j: (0, 0))
    return pl.pallas_call(
        functools.partial(_mlp_ln_kernel, alpha=alpha),
        out_shape=(jax.ShapeDtypeStruct((m, d), F32), jax.ShapeDtypeStruct((m, d), BF16)),
        grid=(m // tm, f // tf),
        in_specs=[row(),
                  pl.BlockSpec((d, tf), lambda i, j: (0, j)),
                  pl.BlockSpec((tf, d), lambda i, j: (j, 0)),
                  row(), vec, vec],
        out_specs=(row(), row()),
        scratch_shapes=[pltpu.VMEM((tm, d), F32)],
        compiler_params=pltpu.CompilerParams(
            dimension_semantics=("parallel", "arbitrary"),
            vmem_limit_bytes=V7X_VMEM_LIMIT_BYTES),
        name="mlp_ln",
    )(xb, w_up, w_down, x, g.reshape(1, d), b.reshape(1, d))


def _rope_tables(seq):
    half = ROT_DIM // 2
    inv_freq = ROPE_THETA ** (-jnp.arange(0, ROT_DIM, 2, dtype=F32) / ROT_DIM)
    ang = jnp.arange(seq, dtype=F32)[:, None] * inv_freq[None, :]
    cos, sin = jnp.cos(ang), jnp.sin(ang)
    zeros = lambda n: jnp.zeros((seq, n), F32)
    c = jnp.concatenate([cos, cos, jnp.ones((seq, HEAD_DIM - ROT_DIM), F32)], axis=1)
    sa = jnp.concatenate([-sin, zeros(HEAD_DIM - half)], axis=1)
    sb = jnp.concatenate([zeros(half), sin, zeros(HEAD_DIM - ROT_DIM)], axis=1)
    return c, sa, sb


def kernel(x, ln_g, ln_b, sb_w_qkv, sb_w_o, kv_w, diff_w_q, diff_lambda, diff_subln_g,
           diff_w_o, mlp_w_up, mlp_w_down):
    batch, seq, d_model = x.shape
    depth = ln_g.shape[0]
    n_a = sb_w_qkv.shape[0]
    assert depth - n_a == 1
    alpha = (2 * depth) ** 0.25
    sb_heads = d_model // HEAD_DIM
    diff_heads = d_model // DIFF_V_DIM
    rope_tabs = _rope_tables(seq)

    xf = x.reshape(batch * seq, d_model)
    xb = xf
    kv_wb = kv_w.astype(BF16)
    for layer in range(depth):
        if layer < n_a:
            qkv = _project(xb, sb_w_qkv[layer].astype(BF16), rope_tabs, 0, d_model, seq)
            attn = _stick_breaking_attention(qkv, batch, seq, sb_heads)
            w_o = sb_w_o[layer]
        else:
            j = layer - n_a
            w_cat = jnp.concatenate([diff_w_q[j].astype(BF16), kv_wb], axis=1)
            qkv = _project(xb, w_cat, rope_tabs, 2 * d_model, d_model, seq)
            lambda_init = 0.8 - 0.6 * math.exp(-0.3 * layer)
            attn = _differential_attention(qkv, diff_lambda[j], diff_subln_g[j], batch, seq,
                                           diff_heads, lambda_init)
            w_o = diff_w_o[j]
        xf, xb = _out_proj_ln(attn, w_o.astype(BF16), xf, ln_g[layer, 0], ln_b[layer, 0], alpha)
        xf, xb = _mlp_ln(xb, mlp_w_up[layer].astype(BF16), mlp_w_down[layer].astype(BF16), xf,
                         ln_g[layer, 1], ln_b[layer, 1], alpha)
    return xf.reshape(batch, seq, d_model)
```

```python
import functools
import math

import jax
import jax.numpy as jnp
from jax import lax
from jax.experimental import pallas as pl
from jax.experimental.pallas import tpu as pltpu

HEAD_DIM = 128
DIFF_V_DIM = 256
ROT_DIM = HEAD_DIM // 4
ROPE_THETA = 500000.0
LN_EPS = 1e-5
SUBLN_EPS = 1e-5
LOGIT_SCALE_LOG2 = HEAD_DIM ** -0.5 * math.log2(math.e)
MASKED_LOGIT = -1e30

V7X_VMEM_LIMIT_BYTES = 56 * 1024 * 1024

F32 = jnp.float32
BF16 = jnp.bfloat16


def _nt_dot(a, b):
    return lax.dot_general(a, b, (((1,), (1,)), ((), ())), preferred_element_type=F32)


def _layer_norm_rows(y, g, b):
    mu = jnp.mean(y, axis=-1, keepdims=True)
    yc = y - mu
    var = jnp.mean(yc * yc, axis=-1, keepdims=True)
    return yc * lax.rsqrt(var + LN_EPS) * g + b


def _proj_kernel(x_ref, w_ref, c_ref, sa_ref, sb_ref, o_ref, *, rope_col_blocks, q_col_blocks,
                 q_scale):
    j = pl.program_id(1)
    acc = jnp.dot(x_ref[...].astype(BF16), w_ref[...], preferred_element_type=F32)
    acc = acc * jnp.where(j < q_col_blocks, q_scale, 1.0)
    tn = acc.shape[1]
    if rope_col_blocks == 0:
        o_ref[...] = acc.astype(o_ref.dtype)
        return

    @pl.when(j < rope_col_blocks)
    def _():
        c = c_ref[...]
        sa = sa_ref[...]
        sb = sb_ref[...]
        for g in range(tn // HEAD_DIM):
            t = acc[:, g * HEAD_DIM:(g + 1) * HEAD_DIM]
            rot = (t * c + pltpu.roll(t, HEAD_DIM - ROT_DIM // 2, 1) * sa
                   + pltpu.roll(t, ROT_DIM // 2, 1) * sb)
            o_ref[:, g * HEAD_DIM:(g + 1) * HEAD_DIM] = rot.astype(o_ref.dtype)

    @pl.when(j >= rope_col_blocks)
    def _():
        o_ref[...] = acc.astype(o_ref.dtype)


def _project(x2d, w, rope_tabs, rope_cols, q_cols, seq, *, tm=1024, tn=1024):
    m, k = x2d.shape
    n = w.shape[1]
    assert m % tm == 0 and n % tn == 0 and seq % tm == 0
    assert rope_cols % tn == 0 and q_cols % tn == 0
    row_blocks_per_seq = seq // tm
    tab_spec = pl.BlockSpec((tm, HEAD_DIM), lambda i, j: (i % row_blocks_per_seq, 0))
    return pl.pallas_call(
        functools.partial(_proj_kernel, rope_col_blocks=rope_cols // tn,
                          q_col_blocks=q_cols // tn, q_scale=LOGIT_SCALE_LOG2),
        out_shape=jax.ShapeDtypeStruct((m, n), BF16),
        grid=(m // tm, n // tn),
        in_specs=[
            pl.BlockSpec((tm, k), lambda i, j: (i, 0)),
            pl.BlockSpec((k, tn), lambda i, j: (0, j)),
            tab_spec, tab_spec, tab_spec,
        ],
        out_specs=pl.BlockSpec((tm, tn), lambda i, j: (i, j)),
        compiler_params=pltpu.CompilerParams(
            dimension_semantics=("parallel", "parallel"),
            vmem_limit_bytes=V7X_VMEM_LIMIT_BYTES),
        name="proj",
    )(x2d, w, *rope_tabs)


def _sb_logits_stage(q, k, strict):
    z = _nt_dot(q, k)
    neg_abs = lax.bitcast_convert_type(
        lax.bitcast_convert_type(z, jnp.uint32) | jnp.uint32(0x80000000), F32)
    sp = jnp.maximum(z, 0.0) + jnp.log2(1.0 + jnp.exp2(neg_abs))
    ls = z - sp
    if strict is not None:
        sp = jnp.where(strict, sp, 0.0)
    return ls, sp.astype(BF16), sp[:, :HEAD_DIM]


def _sb_weights_stage(ls, spb, sp0, v, u, acc_ref, carry_ref, r0, strict):
    groups = ls.shape[1] // HEAD_DIM
    excl = jnp.dot(spb, u, preferred_element_type=F32)
    carry = carry_ref[r0:, :]
    a = jnp.exp2(ls - jnp.concatenate([carry] * groups, axis=1) - excl)
    if strict is not None:
        a = jnp.where(strict, a, 0.0)
    acc_ref[r0:, :] += jnp.dot(a.astype(BF16), v, preferred_element_type=F32)
    carry_ref[r0:, :] = carry + (excl[:, :1] + sp0[:, :1])


def _sb_kernel(q_ref, k_ref, v_ref, u_ref, o_ref, acc_ref, carry_ref, ls_ref, spb_ref, sp0_ref,
               *, blk, chunks):
    qi = pl.program_id(2)
    u = u_ref[...]
    base = qi * chunks

    def k_tile(t):
        return k_ref[pl.ds(pl.multiple_of(t * blk, blk), blk), :]

    def v_tile(t):
        return v_ref[pl.ds(pl.multiple_of(t * blk, blk), blk), :]

    acc_ref[...] = jnp.zeros(acc_ref.shape, F32)
    carry_ref[...] = jnp.zeros(carry_ref.shape, F32)

    for t in reversed(range(chunks)):
        r0 = t * blk
        row = lax.broadcasted_iota(jnp.int32, (blk * chunks - r0, blk), 0)
        col = lax.broadcasted_iota(jnp.int32, (blk * chunks - r0, blk), 1)
        strict = col < row
        ls, spb, sp0 = _sb_logits_stage(q_ref[r0:, :], k_tile(base + t), strict)
        _sb_weights_stage(ls, spb, sp0, v_tile(base + t), u, acc_ref, carry_ref, r0, strict)

    def logits_to_slot(t, slot):
        ls_ref[slot], spb_ref[slot], sp0_ref[slot] = _sb_logits_stage(q_ref[...], k_tile(t), None)

    def weights_from_slot(t, slot):
        _sb_weights_stage(ls_ref[slot], spb_ref[slot], sp0_ref[slot], v_tile(t), u,
                          acc_ref, carry_ref, 0, None)

    @pl.when(base > 0)
    def _():
        logits_to_slot(base - 1, 0)

        def body(i, c):
            t = base - 1 - 2 * i
            weights_from_slot(t, 0)
            logits_to_slot(t - 1, 1)
            weights_from_slot(t - 1, 1)
            logits_to_slot(t - 2, 0)
            return c

        lax.fori_loop(0, qi * (chunks // 2) - 1, body, 0)
        weights_from_slot(1, 0)
        logits_to_slot(0, 1)
        weights_from_slot(0, 1)

    o_ref[...] = acc_ref[...].astype(o_ref.dtype)


def _stick_breaking_attention(qkv, batch, seq, heads, *, blk=256, chunks=4):
    m = qkv.shape[0]
    bq = blk * chunks
    nq = seq // bq
    idx = jnp.arange(blk)
    u = (idx[:, None] > idx[None, :]).astype(BF16)
    return pl.pallas_call(
        functools.partial(_sb_kernel, blk=blk, chunks=chunks),
        out_shape=jax.ShapeDtypeStruct((m, heads * HEAD_DIM), BF16),
        grid=(batch, heads, nq),
        in_specs=[
            pl.BlockSpec((bq, HEAD_DIM), lambda b, h, i: (b * nq + i, h)),
            pl.BlockSpec((seq, HEAD_DIM), lambda b, h, i: (b, heads + h)),
            pl.BlockSpec((seq, HEAD_DIM), lambda b, h, i: (b, 2 * heads + h)),
            pl.BlockSpec((blk, blk), lambda b, h, i: (0, 0)),
        ],
        out_specs=pl.BlockSpec((bq, HEAD_DIM), lambda b, h, i: (b * nq + i, h)),
        scratch_shapes=[pltpu.VMEM((bq, HEAD_DIM), F32),
                        pltpu.VMEM((bq, HEAD_DIM), F32),
                        pltpu.VMEM((2, bq, blk), F32),
                        pltpu.VMEM((2, bq, blk), BF16),
                        pltpu.VMEM((2, bq, HEAD_DIM), F32)],
        compiler_params=pltpu.CompilerParams(
            dimension_semantics=("parallel", "parallel", "parallel"),
            vmem_limit_bytes=V7X_VMEM_LIMIT_BYTES),
        name="sb_attn",
    )(qkv, qkv, qkv, u)


def _diff_kernel(q_ref, k_ref, v_ref, lam_ref, g_ref, o_ref, m_ref, l_ref, acc_ref, *,
                 blk, chunks, lambda_init):
    qi = pl.program_id(2)
    bq = blk * chunks
    wide = 2 * blk
    assert chunks % 2 == 0

    def scores(c, r0, k0, width, masked):
        lanes = slice(c * HEAD_DIM, (c + 1) * HEAD_DIM)
        s = _nt_dot(q_ref[r0:, lanes], k_ref[pl.ds(k0, width), lanes])
        if masked:
            row = lax.broadcasted_iota(jnp.int32, s.shape, 0)
            col = lax.broadcasted_iota(jnp.int32, s.shape, 1)
            s = jnp.where(col <= row, s, MASKED_LOGIT)
        return [s[:, g * HEAD_DIM:(g + 1) * HEAD_DIM] for g in range(width // HEAD_DIM)]

    def sweep(tile_fn):
        for t in range(chunks):
            tile_fn(t * blk, pl.multiple_of((qi * chunks + t) * blk, blk), blk, True)

        def body(i, carry):
            tile_fn(0, pl.multiple_of(i * wide, wide), wide, False)
            return carry

        lax.fori_loop(0, qi * (chunks // 2), body, 0)

    def max_tile(r0, k0, width, masked):
        for c in range(2):
            m_ref[c, r0:, :] = functools.reduce(
                jnp.maximum, scores(c, r0, k0, width, masked), m_ref[c, r0:, :])

    def pv_tile(r0, k0, width, masked):
        v = v_ref[pl.ds(k0, width), :]
        for c in range(2):
            m = m_ref[c, r0:, :]
            p = [jnp.exp2(s - m) for s in scores(c, r0, k0, width, masked)]
            l_ref[c, r0:, :] += functools.reduce(jnp.add, p)
            acc_ref[c, r0:, :] += jnp.dot(jnp.concatenate(p, axis=1).astype(BF16), v,
                                          preferred_element_type=F32)

    m_ref[...] = jnp.full(m_ref.shape, MASKED_LOGIT, F32)
    sweep(max_tile)
    for c in range(2):
        m_ref[c] = jnp.broadcast_to(jnp.max(m_ref[c], axis=1, keepdims=True), m_ref.shape[1:])
    l_ref[...] = jnp.zeros(l_ref.shape, F32)
    acc_ref[...] = jnp.zeros(acc_ref.shape, F32)
    sweep(pv_tile)

    lam = lam_ref[...]
    lam_full = (jnp.exp(jnp.sum(lam[0:1] * lam[1:2], axis=1, keepdims=True))
                - jnp.exp(jnp.sum(lam[2:3] * lam[3:4], axis=1, keepdims=True)) + lambda_init)
    inv_l = [1.0 / jnp.sum(l_ref[c], axis=1, keepdims=True) for c in range(2)]
    o = acc_ref[0] * inv_l[0] - lam_full * (acc_ref[1] * inv_l[1])
    o = o * lax.rsqrt(jnp.mean(o * o, axis=-1, keepdims=True) + SUBLN_EPS)
    o_ref[...] = (o * (g_ref[...] * (1.0 - lambda_init))).astype(o_ref.dtype)


def _differential_attention(qkv, lam, subln_g, batch, seq, heads, lambda_init, *, blk=256,
                            chunks=4):
    m = qkv.shape[0]
    bq = blk * chunks
    nq = seq // bq
    w = DIFF_V_DIM
    return pl.pallas_call(
        functools.partial(_diff_kernel, blk=blk, chunks=chunks, lambda_init=lambda_init),
        out_shape=jax.ShapeDtypeStruct((m, heads * w), BF16),
        grid=(batch, heads, nq),
        in_specs=[
            pl.BlockSpec((bq, w), lambda b, h, i: (b * nq + i, h)),
            pl.BlockSpec((seq, w), lambda b, h, i: (b, heads + h)),
            pl.BlockSpec((seq, w), lambda b, h, i: (b, 2 * heads + h)),
            pl.BlockSpec((4, HEAD_DIM), lambda b, h, i: (0, 0)),
            pl.BlockSpec((1, w), lambda b, h, i: (0, 0)),
        ],
        out_specs=pl.BlockSpec((bq, w), lambda b, h, i: (b * nq + i, h)),
        scratch_shapes=[pltpu.VMEM((2, bq, HEAD_DIM), F32), pltpu.VMEM((2, bq, HEAD_DIM), F32),
                        pltpu.VMEM((2, bq, w), F32)],
        compiler_params=pltpu.CompilerParams(
            dimension_semantics=("parallel", "parallel", "parallel"),
            vmem_limit_bytes=V7X_VMEM_LIMIT_BYTES),
        name="diff_attn",
    )(qkv, qkv, qkv, lam, subln_g.reshape(1, w))


def _out_ln_kernel(a_ref, w_ref, x_ref, g_ref, b_ref, of_ref, ob_ref, *, alpha):
    h = jnp.dot(a_ref[...], w_ref[...], preferred_element_type=F32)
    y = _layer_norm_rows(alpha * x_ref[...] + h, g_ref[...], b_ref[...])
    of_ref[...] = y
    ob_ref[...] = y.astype(BF16)


def _out_proj_ln(attn, w, x, g, b, alpha, *, tm=512):
    m, k = attn.shape
    d = w.shape[1]
    row = pl.BlockSpec((tm, d), lambda i: (i, 0))
    vec = pl.BlockSpec((1, d), lambda i: (0, 0))
    return pl.pallas_call(
        functools.partial(_out_ln_kernel, alpha=alpha),
        out_shape=(jax.ShapeDtypeStruct((m, d), F32), jax.ShapeDtypeStruct((m, d), BF16)),
        grid=(m // tm,),
        in_specs=[pl.BlockSpec((tm, k), lambda i: (i, 0)),
                  pl.BlockSpec((k, d), lambda i: (0, 0)),
                  row, vec, vec],
        out_specs=(row, row),
        compiler_params=pltpu.CompilerParams(
            dimension_semantics=("parallel",),
            vmem_limit_bytes=V7X_VMEM_LIMIT_BYTES),
        name="out_proj_ln",
    )(attn, w, x, g.reshape(1, d), b.reshape(1, d))


def _mlp_ln_kernel(xb_ref, wu_ref, wd_ref, x_ref, g_ref, b_ref, of_ref, ob_ref, acc_ref, *, alpha):
    j = pl.program_id(1)

    @pl.when(j == 0)
    def _():
        acc_ref[...] = jnp.zeros_like(acc_ref)

    h = jnp.maximum(jnp.dot(xb_ref[...], wu_ref[...], preferred_element_type=F32), 0.0)
    acc_ref[...] += jnp.dot((h * h).astype(BF16), wd_ref[...], preferred_element_type=F32)

    @pl.when(j == pl.num_programs(1) - 1)
    def _():
        y = _layer_norm_rows(alpha * x_ref[...] + acc_ref[...], g_ref[...], b_ref[...])
        of_ref[...] = y
        ob_ref[...] = y.astype(BF16)


def _mlp_ln(xb, w_up, w_down, x, g, b, alpha, *, tm=512, tf=512):
    m, d = xb.shape
    f = w_up.shape[1]
    row = lambda: pl.BlockSpec((tm, d), lambda i, j: (i, 0))
    vec = pl.BlockSpec((1, d), lambda i, j: (0, 0))
    return pl.pallas_call(
        functools.partial(_mlp_ln_kernel, alpha=alpha),
        out_shape=(jax.ShapeDtypeStruct((m, d), F32), jax.ShapeDtypeStruct((m, d), BF16)),
        grid=(m // tm, f // tf),
        in_specs=[row(),
                  pl.BlockSpec((d, tf), lambda i, j: (0, j)),
                  pl.BlockSpec((tf, d), lambda i, j: (j, 0)),
                  row(), vec, vec],
        out_specs=(row(), row()),
        scratch_shapes=[pltpu.VMEM((tm, d), F32)],
        compiler_params=pltpu.CompilerParams(
            dimension_semantics=("parallel", "arbitrary"),
            vmem_limit_bytes=V7X_VMEM_LIMIT_BYTES),
        name="mlp_ln",
    )(xb, w_up, w_down, x, g.reshape(1, d), b.reshape(1, d))


def _rope_tables(seq):
    half = ROT_DIM // 2
    inv_freq = ROPE_THETA ** (-jnp.arange(0, ROT_DIM, 2, dtype=F32) / ROT_DIM)
    ang = jnp.arange(seq, dtype=F32)[:, None] * inv_freq[None, :]
    cos, sin = jnp.cos(ang), jnp.sin(ang)
    zeros = lambda n: jnp.zeros((seq, n), F32)
    c = jnp.concatenate([cos, cos, jnp.ones((seq, HEAD_DIM - ROT_DIM), F32)], axis=1)
    sa = jnp.concatenate([-sin, zeros(HEAD_DIM - half)], axis=1)
    sb = jnp.concatenate([zeros(half), sin, zeros(HEAD_DIM - ROT_DIM)], axis=1)
    return c, sa, sb


def kernel(x, ln_g, ln_b, sb_w_qkv, sb_w_o, kv_w, diff_w_q, diff_lambda, diff_subln_g,
           diff_w_o, mlp_w_up, mlp_w_down):
    batch, seq, d_model = x.shape
    depth = ln_g.shape[0]
    n_a = sb_w_qkv.shape[0]
    assert depth - n_a == 1
    alpha = (2 * depth) ** 0.25
    sb_heads = d_model // HEAD_DIM
    diff_heads = d_model // DIFF_V_DIM
    rope_tabs = _rope_tables(seq)

    xf = x.reshape(batch * seq, d_model)
    xb = xf
    kv_wb = kv_w.astype(BF16)
    for layer in range(depth):
        if layer < n_a:
            qkv = _project(xb, sb_w_qkv[layer].astype(BF16), rope_tabs, 0, d_model, seq)
            attn = _stick_breaking_attention(qkv, batch, seq, sb_heads)
            w_o = sb_w_o[layer]
        else:
            j = layer - n_a
            w_cat = jnp.concatenate([diff_w_q[j].astype(BF16), kv_wb], axis=1)
            qkv = _project(xb, w_cat, rope_tabs, 2 * d_model, d_model, seq)
            lambda_init = 0.8 - 0.6 * math.exp(-0.3 * layer)
            attn = _differential_attention(qkv, diff_lambda[j], diff_subln_g[j], batch, seq,
                                           diff_heads, lambda_init)
            w_o = diff_w_o[j]
        xf, xb = _out_proj_ln(attn, w_o.astype(BF16), xf, ln_g[layer, 0], ln_b[layer, 0], alpha)
        xf, xb = _mlp_ln(xb, mlp_w_up[layer].astype(BF16), mlp_w_down[layer].astype(BF16), xf,
                         ln_g[layer, 1], ln_b[layer, 1], alpha)
    return xf.reshape(batch, seq, d_model)
```

```python
import functools
import math

import jax
import jax.numpy as jnp
from jax import lax
from jax.experimental import pallas as pl
from jax.experimental.pallas import tpu as pltpu

HEAD_DIM = 128
DIFF_V_DIM = 256
ROT_DIM = HEAD_DIM // 4
ROPE_THETA = 500000.0
LN_EPS = 1e-5
SUBLN_EPS = 1e-5
LOGIT_SCALE_LOG2 = HEAD_DIM ** -0.5 * math.log2(math.e)
MASKED_LOGIT = -1e30
SHIFT_SLACK_LOG2 = 100.0
EXP2_ARG_MAX = 126.0

V7X_VMEM_LIMIT_BYTES = 56 * 1024 * 1024

F32 = jnp.float32
BF16 = jnp.bfloat16


def _nt_dot(a, b):
    return lax.dot_general(a, b, (((1,), (1,)), ((), ())), preferred_element_type=F32)


def _layer_norm_rows(y, g, b):
    mu = jnp.mean(y, axis=-1, keepdims=True)
    yc = y - mu
    var = jnp.mean(yc * yc, axis=-1, keepdims=True)
    return yc * lax.rsqrt(var + LN_EPS) * g + b


def _proj_kernel(x_ref, w_ref, c_ref, sa_ref, sb_ref, o_ref, *, rope_col_blocks, q_col_blocks,
                 q_scale):
    j = pl.program_id(1)
    acc = jnp.dot(x_ref[...].astype(BF16), w_ref[...], preferred_element_type=F32)
    acc = acc * jnp.where(j < q_col_blocks, q_scale, 1.0)
    tn = acc.shape[1]
    if rope_col_blocks == 0:
        o_ref[...] = acc.astype(o_ref.dtype)
        return

    @pl.when(j < rope_col_blocks)
    def _():
        c = c_ref[...]
        sa = sa_ref[...]
        sb = sb_ref[...]
        for g in range(tn // HEAD_DIM):
            t = acc[:, g * HEAD_DIM:(g + 1) * HEAD_DIM]
            rot = (t * c + pltpu.roll(t, HEAD_DIM - ROT_DIM // 2, 1) * sa
                   + pltpu.roll(t, ROT_DIM // 2, 1) * sb)
            o_ref[:, g * HEAD_DIM:(g + 1) * HEAD_DIM] = rot.astype(o_ref.dtype)

    @pl.when(j >= rope_col_blocks)
    def _():
        o_ref[...] = acc.astype(o_ref.dtype)


def _project(x2d, w, rope_tabs, rope_cols, q_cols, seq, *, tm=1024, tn=1024):
    m, k = x2d.shape
    n = w.shape[1]
    assert m % tm == 0 and n % tn == 0 and seq % tm == 0
    assert rope_cols % tn == 0 and q_cols % tn == 0
    row_blocks_per_seq = seq // tm
    tab_spec = pl.BlockSpec((tm, HEAD_DIM), lambda i, j: (i % row_blocks_per_seq, 0))
    return pl.pallas_call(
        functools.partial(_proj_kernel, rope_col_blocks=rope_cols // tn,
                          q_col_blocks=q_cols // tn, q_scale=LOGIT_SCALE_LOG2),
        out_shape=jax.ShapeDtypeStruct((m, n), BF16),
        grid=(m // tm, n // tn),
        in_specs=[
            pl.BlockSpec((tm, k), lambda i, j: (i, 0)),
            pl.BlockSpec((k, tn), lambda i, j: (0, j)),
            tab_spec, tab_spec, tab_spec,
        ],
        out_specs=pl.BlockSpec((tm, tn), lambda i, j: (i, j)),
        compiler_params=pltpu.CompilerParams(
            dimension_semantics=("parallel", "parallel"),
            vmem_limit_bytes=V7X_VMEM_LIMIT_BYTES),
        name="proj",
    )(x2d, w, *rope_tabs)


def _sb_logits_stage(q, k, strict):
    z = _nt_dot(q, k)
    sp = jnp.maximum(z, jnp.log2(1.0 + jnp.exp2(jnp.minimum(z, EXP2_ARG_MAX))))
    ls = z - sp
    if strict is not None:
        sp = jnp.where(strict, sp, 0.0)
    return ls, sp.astype(BF16), sp[:, :HEAD_DIM]


def _sb_weights_stage(ls, spb, sp0, v, u, acc_ref, carry_ref, r0, strict):
    groups = ls.shape[1] // HEAD_DIM
    excl = jnp.dot(spb, u, preferred_element_type=F32)
    carry = carry_ref[r0:, :]
    a = jnp.exp2(ls - jnp.concatenate([carry] * groups, axis=1) - excl)
    if strict is not None:
        a = jnp.where(strict, a, 0.0)
    acc_ref[r0:, :] += jnp.dot(a.astype(BF16), v, preferred_element_type=F32)
    carry_ref[r0:, :] = carry + (excl[:, :1] + sp0[:, :1])


def _sb_kernel(q_ref, k_ref, v_ref, u_ref, o_ref, acc_ref, carry_ref, ls_ref, spb_ref, sp0_ref,
               *, blk, chunks):
    assert chunks % 2 == 0
    qi = pl.program_id(2)
    u = u_ref[...]
    base = qi * chunks

    def k_tile(t):
        return k_ref[pl.ds(pl.multiple_of(t * blk, blk), blk), :]

    def v_tile(t):
        return v_ref[pl.ds(pl.multiple_of(t * blk, blk), blk), :]

    acc_ref[...] = jnp.zeros(acc_ref.shape, F32)
    carry_ref[...] = jnp.zeros(carry_ref.shape, F32)

    for t in reversed(range(chunks)):
        r0 = t * blk
        row = lax.broadcasted_iota(jnp.int32, (blk * chunks - r0, blk), 0)
        col = lax.broadcasted_iota(jnp.int32, (blk * chunks - r0, blk), 1)
        strict = col < row
        ls, spb, sp0 = _sb_logits_stage(q_ref[r0:, :], k_tile(base + t), strict)
        _sb_weights_stage(ls, spb, sp0, v_tile(base + t), u, acc_ref, carry_ref, r0, strict)

    def logits_to_slot(t, slot):
        ls_ref[slot], spb_ref[slot], sp0_ref[slot] = _sb_logits_stage(q_ref[...], k_tile(t), None)

    def weights_from_slot(t, slot):
        _sb_weights_stage(ls_ref[slot], spb_ref[slot], sp0_ref[slot], v_tile(t), u,
                          acc_ref, carry_ref, 0, None)

    @pl.when(base > 0)
    def _():
        logits_to_slot(base - 1, 0)

        def group(first, is_last):
            for j in range(chunks):
                weights_from_slot(first - j, j % 2)
                if not (is_last and j == chunks - 1):
                    logits_to_slot(first - j - 1, (j + 1) % 2)

        def body(i, c):
            group(base - 1 - chunks * i, False)
            return c

        lax.fori_loop(0, qi - 1, body, 0)
        group(chunks - 1, True)

    o_ref[...] = acc_ref[...].astype(o_ref.dtype)


def _stick_breaking_attention(qkv, batch, seq, heads, *, blk=256, chunks=4):
    m = qkv.shape[0]
    bq = blk * chunks
    nq = seq // bq
    idx = jnp.arange(blk)
    u = (idx[:, None] > idx[None, :]).astype(BF16)
    return pl.pallas_call(
        functools.partial(_sb_kernel, blk=blk, chunks=chunks),
        out_shape=jax.ShapeDtypeStruct((m, heads * HEAD_DIM), BF16),
        grid=(batch, heads, nq),
        in_specs=[
            pl.BlockSpec((bq, HEAD_DIM), lambda b, h, i: (b * nq + i, h)),
            pl.BlockSpec((seq, HEAD_DIM), lambda b, h, i: (b, heads + h)),
            pl.BlockSpec((seq, HEAD_DIM), lambda b, h, i: (b, 2 * heads + h)),
            pl.BlockSpec((blk, blk), lambda b, h, i: (0, 0)),
        ],
        out_specs=pl.BlockSpec((bq, HEAD_DIM), lambda b, h, i: (b * nq + i, h)),
        scratch_shapes=[pltpu.VMEM((bq, HEAD_DIM), F32),
                        pltpu.VMEM((bq, HEAD_DIM), F32),
                        pltpu.VMEM((2, bq, blk), F32),
                        pltpu.VMEM((2, bq, blk), BF16),
                        pltpu.VMEM((2, bq, HEAD_DIM), F32)],
        compiler_params=pltpu.CompilerParams(
            dimension_semantics=("parallel", "parallel", "parallel"),
            vmem_limit_bytes=V7X_VMEM_LIMIT_BYTES),
        name="sb_attn",
    )(qkv, qkv, qkv, u)


def _diff_kernel(q_ref, k_ref, v_ref, lam_ref, g_ref, o_ref, m_ref, l_ref, acc_ref, kmax_ref, *,
                 blk, chunks, lambda_init):
    qi = pl.program_id(2)
    bq = blk * chunks
    wide = 2 * blk
    assert chunks % 2 == 0

    def scores(c, r0, k0, width, masked):
        lanes = slice(c * HEAD_DIM, (c + 1) * HEAD_DIM)
        s = _nt_dot(q_ref[r0:, lanes], k_ref[pl.ds(k0, width), lanes])
        if masked:
            row = lax.broadcasted_iota(jnp.int32, s.shape, 0)
            col = lax.broadcasted_iota(jnp.int32, s.shape, 1)
            s = jnp.where(col <= row, s, MASKED_LOGIT)
        return [s[:, g * HEAD_DIM:(g + 1) * HEAD_DIM] for g in range(width // HEAD_DIM)]

    def sweep(tile_fn):
        for t in range(chunks):
            tile_fn(t * blk, pl.multiple_of((qi * chunks + t) * blk, blk), blk, True)

        def body(i, carry):
            tile_fn(0, pl.multiple_of(i * wide, wide), wide, False)
            return carry

        lax.fori_loop(0, qi * (chunks // 2), body, 0)

    def max_tile(r0, k0, width, masked):
        for c in range(2):
            m_ref[c, r0:, :] = functools.reduce(
                jnp.maximum, scores(c, r0, k0, width, masked), m_ref[c, r0:, :])

    def pv_tile(r0, k0, width, masked):
        v = v_ref[pl.ds(k0, width), :]
        for c in range(2):
            m = m_ref[c, r0:, :]
            p = [jnp.exp2(s - m) for s in scores(c, r0, k0, width, masked)]
            l_ref[c, r0:, :] += functools.reduce(jnp.add, p)
            acc_ref[c, r0:, :] += jnp.dot(jnp.concatenate(p, axis=1).astype(BF16), v,
                                          preferred_element_type=F32)

    @pl.when(qi == 0)
    def _():
        def body(i, best):
            kf = k_ref[pl.ds(pl.multiple_of(i * wide, wide), wide), :].astype(F32)
            sq = kf * kf
            return tuple(
                jnp.maximum(best[c], jnp.sum(sq[:, c * HEAD_DIM:(c + 1) * HEAD_DIM], axis=1,
                                             keepdims=True)) for c in range(2))

        best = lax.fori_loop(0, k_ref.shape[0] // wide, body,
                             (jnp.zeros((wide, 1), F32),) * 2)
        for c in range(2):
            kmax_ref[c] = jnp.broadcast_to(jnp.max(best[c], axis=0, keepdims=True),
                                           kmax_ref.shape[1:])

    qf = q_ref[...].astype(F32)
    worst = []
    for c in range(2):
        lanes = slice(c * HEAD_DIM, (c + 1) * HEAD_DIM)
        q_sq = jnp.sum(qf[:, lanes] * qf[:, lanes], axis=1, keepdims=True)
        bound = jnp.sqrt(q_sq * kmax_ref[c, :1, :1])
        m_ref[c] = jnp.broadcast_to(bound, m_ref.shape[1:])
        worst.append(jnp.max(bound))
    bound_ok = 2.0 * jnp.maximum(worst[0], worst[1]) <= SHIFT_SLACK_LOG2

    @pl.when(jnp.logical_not(bound_ok))
    def _():
        m_ref[...] = jnp.full(m_ref.shape, MASKED_LOGIT, F32)
        sweep(max_tile)
        for c in range(2):
            m_ref[c] = jnp.broadcast_to(jnp.max(m_ref[c], axis=1, keepdims=True),
                                        m_ref.shape[1:])

    l_ref[...] = jnp.zeros(l_ref.shape, F32)
    acc_ref[...] = jnp.zeros(acc_ref.shape, F32)
    sweep(pv_tile)

    lam = lam_ref[...]
    lam_full = (jnp.exp(jnp.sum(lam[0:1] * lam[1:2], axis=1, keepdims=True))
                - jnp.exp(jnp.sum(lam[2:3] * lam[3:4], axis=1, keepdims=True)) + lambda_init)
    inv_l = [1.0 / jnp.sum(l_ref[c], axis=1, keepdims=True) for c in range(2)]
    o = acc_ref[0] * inv_l[0] - lam_full * (acc_ref[1] * inv_l[1])
    o = o * lax.rsqrt(jnp.mean(o * o, axis=-1, keepdims=True) + SUBLN_EPS)
    o_ref[...] = (o * (g_ref[...] * (1.0 - lambda_init))).astype(o_ref.dtype)


def _differential_attention(qkv, lam, subln_g, batch, seq, heads, lambda_init, *, blk=256,
                            chunks=4):
    m = qkv.shape[0]
    bq = blk * chunks
    nq = seq // bq
    w = DIFF_V_DIM
    return pl.pallas_call(
        functools.partial(_diff_kernel, blk=blk, chunks=chunks, lambda_init=lambda_init),
        out_shape=jax.ShapeDtypeStruct((m, heads * w), BF16),
        grid=(batch, heads, nq),
        in_specs=[
            pl.BlockSpec((bq, w), lambda b, h, i: (b * nq + i, h)),
            pl.BlockSpec((seq, w), lambda b, h, i: (b, heads + h)),
            pl.BlockSpec((seq, w), lambda b, h, i: (b, 2 * heads + h)),
            pl.BlockSpec((4, HEAD_DIM), lambda b, h, i: (0, 0)),
            pl.BlockSpec((1, w), lambda b, h, i: (0, 0)),
        ],
        out_specs=pl.BlockSpec((bq, w), lambda b, h, i: (b * nq + i, h)),
        scratch_shapes=[pltpu.VMEM((2, bq, HEAD_DIM), F32), pltpu.VMEM((2, bq, HEAD_DIM), F32),
                        pltpu.VMEM((2, bq, w), F32), pltpu.VMEM((2, 8, HEAD_DIM), F32)],
        compiler_params=pltpu.CompilerParams(
            dimension_semantics=("parallel", "parallel", "arbitrary"),
            vmem_limit_bytes=V7X_VMEM_LIMIT_BYTES),
        name="diff_attn",
    )(qkv, qkv, qkv, lam, subln_g.reshape(1, w))


def _out_ln_kernel(a_ref, w_ref, x_ref, g_ref, b_ref, of_ref, ob_ref, *, alpha):
    h = jnp.dot(a_ref[...], w_ref[...], preferred_element_type=F32)
    y = _layer_norm_rows(alpha * x_ref[...] + h, g_ref[...], b_ref[...])
    of_ref[...] = y
    ob_ref[...] = y.astype(BF16)


def _out_proj_ln(attn, w, x, g, b, alpha, *, tm=512):
    m, k = attn.shape
    d = w.shape[1]
    row = pl.BlockSpec((tm, d), lambda i: (i, 0))
    vec = pl.BlockSpec((1, d), lambda i: (0, 0))
    return pl.pallas_call(
        functools.partial(_out_ln_kernel, alpha=alpha),
        out_shape=(jax.ShapeDtypeStruct((m, d), F32), jax.ShapeDtypeStruct((m, d), BF16)),
        grid=(m // tm,),
        in_specs=[pl.BlockSpec((tm, k), lambda i: (i, 0)),
                  pl.BlockSpec((k, d), lambda i: (0, 0)),
                  row, vec, vec],
        out_specs=(row, row),
        compiler_params=pltpu.CompilerParams(
            dimension_semantics=("parallel",),
            vmem_limit_bytes=V7X_VMEM_LIMIT_BYTES),
        name="out_proj_ln",
    )(attn, w, x, g.reshape(1, d), b.reshape(1, d))


def _mlp_ln_kernel(xb_ref, wu_ref, wd_ref, x_ref, g_ref, b_ref, of_ref, ob_ref, acc_ref, *, alpha):
    j = pl.program_id(1)

    @pl.when(j == 0)
    def _():
        acc_ref[...] = jnp.zeros_like(acc_ref)

    h = jnp.maximum(jnp.dot(xb_ref[...], wu_ref[...], preferred_element_type=F32), 0.0)
    acc_ref[...] += jnp.dot((h * h).astype(BF16), wd_ref[...], preferred_element_type=F32)

    @pl.when(j == pl.num_programs(1) - 1)
    def _():
        y = _layer_norm_rows(alpha * x_ref[...] + acc_ref[...], g_ref[...], b_ref[...])
        of_ref[...] = y
        ob_ref[...] = y.astype(BF16)


def _mlp_ln(xb, w_up, w_down, x, g, b, alpha, *, tm=512, tf=512):
    m, d = xb.shape
    f = w_up.shape[1]
    row = lambda: pl.BlockSpec((tm, d), lambda i, j: (i, 0))
    vec = pl.BlockSpec((1, d), lambda i, j: (0, 0))
    return pl.pallas_call(
        functools.partial(_mlp_ln_kernel, alpha=alpha),
        out_shape=(jax.ShapeDtypeStruct((m, d), F32), jax.ShapeDtypeStruct((m, d), BF16)),
        grid=(m // tm, f // tf),
        in_specs=[row(),
                  pl.BlockSpec((d, tf), lambda i, j: (0, j)),
                  pl.BlockSpec((tf, d), lambda i, j: (j, 0)),
                  row(), vec, vec],
        out_specs=(row(), row()),
        scratch_shapes=[pltpu.VMEM((tm, d), F32)],
        compiler_params=pltpu.CompilerParams(
            dimension_semantics=("parallel", "arbitrary"),
            vmem_limit_bytes=V7X_VMEM_LIMIT_BYTES),
        name="mlp_ln",
    )(xb, w_up, w_down, x, g.reshape(1, d), b.reshape(1, d))


def _rope_tables(seq):
    half = ROT_DIM // 2
    inv_freq = ROPE_THETA ** (-jnp.arange(0, ROT_DIM, 2, dtype=F32) / ROT_DIM)
    ang = jnp.arange(seq, dtype=F32)[:, None] * inv_freq[None, :]
    cos, sin = jnp.cos(ang), jnp.sin(ang)
    zeros = lambda n: jnp.zeros((seq, n), F32)
    c = jnp.concatenate([cos, cos, jnp.ones((seq, HEAD_DIM - ROT_DIM), F32)], axis=1)
    sa = jnp.concatenate([-sin, zeros(HEAD_DIM - half)], axis=1)
    sb = jnp.concatenate([zeros(half), sin, zeros(HEAD_DIM - ROT_DIM)], axis=1)
    return c, sa, sb


def kernel(x, ln_g, ln_b, sb_w_qkv, sb_w_o, kv_w, diff_w_q, diff_lambda, diff_subln_g,
           diff_w_o, mlp_w_up, mlp_w_down):
    batch, seq, d_model = x.shape
    depth = ln_g.shape[0]
    n_a = sb_w_qkv.shape[0]
    assert depth - n_a == 1
    alpha = (2 * depth) ** 0.25
    sb_heads = d_model // HEAD_DIM
    diff_heads = d_model // DIFF_V_DIM
    rope_tabs = _rope_tables(seq)

    xf = x.reshape(batch * seq, d_model)
    xb = xf
    kv_wb = kv_w.astype(BF16)
    for layer in range(depth):
        if layer < n_a:
            qkv = _project(xb, sb_w_qkv[layer].astype(BF16), rope_tabs, 0, d_model, seq)
            attn = _stick_breaking_attention(qkv, batch, seq, sb_heads)
            w_o = sb_w_o[layer]
        else:
            j = layer - n_a
            w_cat = jnp.concatenate([diff_w_q[j].astype(BF16), kv_wb], axis=1)
            qkv = _project(xb, w_cat, rope_tabs, 2 * d_model, d_model, seq)
            lambda_init = 0.8 - 0.6 * math.exp(-0.3 * layer)
            attn = _differential_attention(qkv, diff_lambda[j], diff_subln_g[j], batch, seq,
                                           diff_heads, lambda_init)
            w_o = diff_w_o[j]
        xf, xb = _out_proj_ln(attn, w_o.astype(BF16), xf, ln_g[layer, 0], ln_b[layer, 0], alpha)
        xf, xb = _mlp_ln(xb, mlp_w_up[layer].astype(BF16), mlp_w_down[layer].astype(BF16), xf,
                         ln_g[layer, 1], ln_b[layer, 1], alpha)
    return xf.reshape(batch, seq, d_model)
```

```python
import functools
import math

import jax
import jax.numpy as jnp
from jax import lax
from jax.experimental import pallas as pl
from jax.experimental.pallas import tpu as pltpu

HEAD_DIM = 128
DIFF_V_DIM = 256
ROT_DIM = HEAD_DIM // 4
ROPE_THETA = 500000.0
LN_EPS = 1e-5
SUBLN_EPS = 1e-5
LOGIT_SCALE_LOG2 = HEAD_DIM ** -0.5 * math.log2(math.e)
MASKED_LOGIT = -1e30
SHIFT_SLACK_LOG2 = 100.0
EXP2_ARG_MAX = 126.0

V7X_VMEM_LIMIT_BYTES = 56 * 1024 * 1024

F32 = jnp.float32
BF16 = jnp.bfloat16


def _nt_dot(a, b):
    return lax.dot_general(a, b, (((1,), (1,)), ((), ())), preferred_element_type=F32)


def _layer_norm_rows(y, g, b):
    mu = jnp.mean(y, axis=-1, keepdims=True)
    yc = y - mu
    var = jnp.mean(yc * yc, axis=-1, keepdims=True)
    return yc * lax.rsqrt(var + LN_EPS) * g + b


EPILOGUE_ROWS = 256


def _proj_kernel(x_ref, w_ref, c_ref, sa_ref, sb_ref, o_ref, *, rope_col_blocks, q_col_blocks,
                 q_scale):
    j = pl.program_id(1)
    tm, tn = o_ref.shape
    scale = jnp.where(j < q_col_blocks, q_scale, 1.0)

    def chunks(epilogue):
        for r0 in range(0, tm, EPILOGUE_ROWS):
            rows = slice(r0, r0 + EPILOGUE_ROWS)
            acc = jnp.dot(x_ref[rows, :].astype(BF16), w_ref[...], preferred_element_type=F32)
            epilogue(rows, acc * scale)

    def plain(rows, acc):
        o_ref[rows, :] = acc.astype(o_ref.dtype)

    def rotary(rows, acc):
        c = c_ref[rows, :]
        sa = sa_ref[rows, :]
        sb = sb_ref[rows, :]
        for g in range(tn // HEAD_DIM):
            t = acc[:, g * HEAD_DIM:(g + 1) * HEAD_DIM]
            rot = (t * c + pltpu.roll(t, HEAD_DIM - ROT_DIM // 2, 1) * sa
                   + pltpu.roll(t, ROT_DIM // 2, 1) * sb)
            o_ref[rows, g * HEAD_DIM:(g + 1) * HEAD_DIM] = rot.astype(o_ref.dtype)

    if rope_col_blocks == 0:
        chunks(plain)
        return

    @pl.when(j < rope_col_blocks)
    def _():
        chunks(rotary)

    @pl.when(j >= rope_col_blocks)
    def _():
        chunks(plain)


def _project(x2d, w, rope_tabs, rope_cols, q_cols, seq, *, tm=1024, tn=1024):
    m, k = x2d.shape
    n = w.shape[1]
    assert m % tm == 0 and n % tn == 0 and seq % tm == 0
    assert rope_cols % tn == 0 and q_cols % tn == 0
    row_blocks_per_seq = seq // tm
    tab_spec = pl.BlockSpec((tm, HEAD_DIM), lambda i, j: (i % row_blocks_per_seq, 0))
    return pl.pallas_call(
        functools.partial(_proj_kernel, rope_col_blocks=rope_cols // tn,
                          q_col_blocks=q_cols // tn, q_scale=LOGIT_SCALE_LOG2),
        out_shape=jax.ShapeDtypeStruct((m, n), BF16),
        grid=(m // tm, n // tn),
        in_specs=[
            pl.BlockSpec((tm, k), lambda i, j: (i, 0)),
            pl.BlockSpec((k, tn), lambda i, j: (0, j)),
            tab_spec, tab_spec, tab_spec,
        ],
        out_specs=pl.BlockSpec((tm, tn), lambda i, j: (i, j)),
        compiler_params=pltpu.CompilerParams(
            dimension_semantics=("parallel", "parallel"),
            vmem_limit_bytes=V7X_VMEM_LIMIT_BYTES),
        name="proj",
    )(x2d, w, *rope_tabs)


def _sb_logits_stage(q, k, strict):
    z = _nt_dot(q, k)
    sp = jnp.maximum(z, jnp.log2(1.0 + jnp.exp2(jnp.minimum(z, EXP2_ARG_MAX))))
    ls = z - sp
    if strict is not None:
        sp = jnp.where(strict, sp, 0.0)
    return ls, sp.astype(BF16), sp[:, :HEAD_DIM]


def _sb_weights_stage(ls, spb, sp0, v, u, acc_ref, carry_ref, r0, strict):
    groups = ls.shape[1] // HEAD_DIM
    excl = jnp.dot(spb, u, preferred_element_type=F32)
    carry = carry_ref[r0:, :]
    a = jnp.exp2(ls - jnp.concatenate([carry] * groups, axis=1) - excl)
    if strict is not None:
        a = jnp.where(strict, a, 0.0)
    acc_ref[r0:, :] += jnp.dot(a.astype(BF16), v, preferred_element_type=F32)
    carry_ref[r0:, :] = carry + (excl[:, :1] + sp0[:, :1])


def _sb_kernel(q_ref, k_ref, v_ref, u_ref, o_ref, acc_ref, carry_ref, ls_ref, spb_ref, sp0_ref,
               *, blk, chunks):
    assert chunks % 2 == 0
    qi = pl.program_id(2)
    u = u_ref[...]
    base = qi * chunks

    def k_tile(t):
        return k_ref[pl.ds(pl.multiple_of(t * blk, blk), blk), :]

    def v_tile(t):
        return v_ref[pl.ds(pl.multiple_of(t * blk, blk), blk), :]

    acc_ref[...] = jnp.zeros(acc_ref.shape, F32)
    carry_ref[...] = jnp.zeros(carry_ref.shape, F32)

    for t in reversed(range(chunks)):
        r0 = t * blk
        row = lax.broadcasted_iota(jnp.int32, (blk * chunks - r0, blk), 0)
        col = lax.broadcasted_iota(jnp.int32, (blk * chunks - r0, blk), 1)
        strict = col < row
        ls, spb, sp0 = _sb_logits_stage(q_ref[r0:, :], k_tile(base + t), strict)
        _sb_weights_stage(ls, spb, sp0, v_tile(base + t), u, acc_ref, carry_ref, r0, strict)

    def logits_to_slot(t, slot):
        ls_ref[slot], spb_ref[slot], sp0_ref[slot] = _sb_logits_stage(q_ref[...], k_tile(t), None)

    def weights_from_slot(t, slot):
        _sb_weights_stage(ls_ref[slot], spb_ref[slot], sp0_ref[slot], v_tile(t), u,
                          acc_ref, carry_ref, 0, None)

    @pl.when(base > 0)
    def _():
        logits_to_slot(base - 1, 0)

        def group(first, is_last):
            for j in range(chunks):
                weights_from_slot(first - j, j % 2)
                if not (is_last and j == chunks - 1):
                    logits_to_slot(first - j - 1, (j + 1) % 2)

        def body(i, c):
            group(base - 1 - chunks * i, False)
            return c

        lax.fori_loop(0, qi - 1, body, 0)
        group(chunks - 1, True)

    o_ref[...] = acc_ref[...].astype(o_ref.dtype)


def _stick_breaking_attention(qkv, batch, seq, heads, *, blk=256, chunks=4):
    m = qkv.shape[0]
    bq = blk * chunks
    nq = seq // bq
    idx = jnp.arange(blk)
    u = (idx[:, None] > idx[None, :]).astype(BF16)
    return pl.pallas_call(
        functools.partial(_sb_kernel, blk=blk, chunks=chunks),
        out_shape=jax.ShapeDtypeStruct((m, heads * HEAD_DIM), BF16),
        grid=(batch, heads, nq),
        in_specs=[
            pl.BlockSpec((bq, HEAD_DIM), lambda b, h, i: (b * nq + i, h)),
            pl.BlockSpec((seq, HEAD_DIM), lambda b, h, i: (b, heads + h)),
            pl.BlockSpec((seq, HEAD_DIM), lambda b, h, i: (b, 2 * heads + h)),
            pl.BlockSpec((blk, blk), lambda b, h, i: (0, 0)),
        ],
        out_specs=pl.BlockSpec((bq, HEAD_DIM), lambda b, h, i: (b * nq + i, h)),
        scratch_shapes=[pltpu.VMEM((bq, HEAD_DIM), F32),
                        pltpu.VMEM((bq, HEAD_DIM), F32),
                        pltpu.VMEM((2, bq, blk), F32),
                        pltpu.VMEM((2, bq, blk), BF16),
                        pltpu.VMEM((2, bq, HEAD_DIM), F32)],
        compiler_params=pltpu.CompilerParams(
            dimension_semantics=("parallel", "parallel", "parallel"),
            vmem_limit_bytes=V7X_VMEM_LIMIT_BYTES),
        name="sb_attn",
    )(qkv, qkv, qkv, u)


def _diff_kernel(q_ref, k_ref, v_ref, lam_ref, g_ref, o_ref, m_ref, l_ref, acc_ref, kmax_ref, *,
                 blk, chunks, lambda_init):
    qi = pl.program_id(2)
    bq = blk * chunks
    wide = 2 * blk
    assert chunks % 2 == 0

    def scores(c, r0, k0, width, masked):
        lanes = slice(c * HEAD_DIM, (c + 1) * HEAD_DIM)
        s = _nt_dot(q_ref[r0:, lanes], k_ref[pl.ds(k0, width), lanes])
        if masked:
            row = lax.broadcasted_iota(jnp.int32, s.shape, 0)
            col = lax.broadcasted_iota(jnp.int32, s.shape, 1)
            s = jnp.where(col <= row, s, MASKED_LOGIT)
        return [s[:, g * HEAD_DIM:(g + 1) * HEAD_DIM] for g in range(width // HEAD_DIM)]

    def sweep(tile_fn):
        for t in range(chunks):
            tile_fn(t * blk, pl.multiple_of((qi * chunks + t) * blk, blk), blk, True)

        def body(i, carry):
            for j in range(chunks // 2):
                tile_fn(0, pl.multiple_of((i * (chunks // 2) + j) * wide, wide), wide, False)
            return carry

        lax.fori_loop(0, qi, body, 0)

    def max_tile(r0, k0, width, masked):
        for c in range(2):
            m_ref[c, r0:, :] = functools.reduce(
                jnp.maximum, scores(c, r0, k0, width, masked), m_ref[c, r0:, :])

    def pv_tile(r0, k0, width, masked):
        v = v_ref[pl.ds(k0, width), :]
        for c in range(2):
            m = m_ref[c, r0:, :]
            p = [jnp.exp2(s - m) for s in scores(c, r0, k0, width, masked)]
            l_ref[c, r0:, :] += functools.reduce(jnp.add, p)
            acc_ref[c, r0:, :] += jnp.dot(jnp.concatenate(p, axis=1).astype(BF16), v,
                                          preferred_element_type=F32)

    @pl.when(qi == 0)
    def _():
        def body(i, best):
            kf = k_ref[pl.ds(pl.multiple_of(i * wide, wide), wide), :].astype(F32)
            sq = kf * kf
            return tuple(
                jnp.maximum(best[c], jnp.sum(sq[:, c * HEAD_DIM:(c + 1) * HEAD_DIM], axis=1,
                                             keepdims=True)) for c in range(2))

        best = lax.fori_loop(0, k_ref.shape[0] // wide, body,
                             (jnp.zeros((wide, 1), F32),) * 2)
        for c in range(2):
            kmax_ref[c] = jnp.broadcast_to(jnp.max(best[c], axis=0, keepdims=True),
                                           kmax_ref.shape[1:])

    qf = q_ref[...].astype(F32)
    worst = []
    for c in range(2):
        lanes = slice(c * HEAD_DIM, (c + 1) * HEAD_DIM)
        q_sq = jnp.sum(qf[:, lanes] * qf[:, lanes], axis=1, keepdims=True)
        bound = jnp.sqrt(q_sq * kmax_ref[c, :1, :1])
        m_ref[c] = jnp.broadcast_to(bound, m_ref.shape[1:])
        worst.append(jnp.max(bound))
    bound_ok = 2.0 * jnp.maximum(worst[0], worst[1]) <= SHIFT_SLACK_LOG2

    @pl.when(jnp.logical_not(bound_ok))
    def _():
        m_ref[...] = jnp.full(m_ref.shape, MASKED_LOGIT, F32)
        sweep(max_tile)
        for c in range(2):
            m_ref[c] = jnp.broadcast_to(jnp.max(m_ref[c], axis=1, keepdims=True),
                                        m_ref.shape[1:])

    l_ref[...] = jnp.zeros(l_ref.shape, F32)
    acc_ref[...] = jnp.zeros(acc_ref.shape, F32)
    sweep(pv_tile)

    lam = lam_ref[...]
    lam_full = (jnp.exp(jnp.sum(lam[0:1] * lam[1:2], axis=1, keepdims=True))
                - jnp.exp(jnp.sum(lam[2:3] * lam[3:4], axis=1, keepdims=True)) + lambda_init)
    inv_l = [1.0 / jnp.sum(l_ref[c], axis=1, keepdims=True) for c in range(2)]
    o = acc_ref[0] * inv_l[0] - lam_full * (acc_ref[1] * inv_l[1])
    o = o * lax.rsqrt(jnp.mean(o * o, axis=-1, keepdims=True) + SUBLN_EPS)
    o_ref[...] = (o * (g_ref[...] * (1.0 - lambda_init))).astype(o_ref.dtype)


def _differential_attention(qkv, lam, subln_g, batch, seq, heads, lambda_init, *, blk=256,
                            chunks=4):
    m = qkv.shape[0]
    bq = blk * chunks
    nq = seq // bq
    w = DIFF_V_DIM
    return pl.pallas_call(
        functools.partial(_diff_kernel, blk=blk, chunks=chunks, lambda_init=lambda_init),
        out_shape=jax.ShapeDtypeStruct((m, heads * w), BF16),
        grid=(batch, heads, nq),
        in_specs=[
            pl.BlockSpec((bq, w), lambda b, h, i: (b * nq + i, h)),
            pl.BlockSpec((seq, w), lambda b, h, i: (b, heads + h)),
            pl.BlockSpec((seq, w), lambda b, h, i: (b, 2 * heads + h)),
            pl.BlockSpec((4, HEAD_DIM), lambda b, h, i: (0, 0)),
            pl.BlockSpec((1, w), lambda b, h, i: (0, 0)),
        ],
        out_specs=pl.BlockSpec((bq, w), lambda b, h, i: (b * nq + i, h)),
        scratch_shapes=[pltpu.VMEM((2, bq, HEAD_DIM), F32), pltpu.VMEM((2, bq, HEAD_DIM), F32),
                        pltpu.VMEM((2, bq, w), F32), pltpu.VMEM((2, 8, HEAD_DIM), F32)],
        compiler_params=pltpu.CompilerParams(
            dimension_semantics=("parallel", "parallel", "arbitrary"),
            vmem_limit_bytes=V7X_VMEM_LIMIT_BYTES),
        name="diff_attn",
    )(qkv, qkv, qkv, lam, subln_g.reshape(1, w))


def _out_ln_kernel(a_ref, w_ref, x_ref, g_ref, b_ref, of_ref, ob_ref, *, alpha):
    for r0 in range(0, of_ref.shape[0], EPILOGUE_ROWS):
        rows = slice(r0, r0 + EPILOGUE_ROWS)
        h = jnp.dot(a_ref[rows, :], w_ref[...], preferred_element_type=F32)
        y = _layer_norm_rows(alpha * x_ref[rows, :] + h, g_ref[...], b_ref[...])
        of_ref[rows, :] = y
        ob_ref[rows, :] = y.astype(BF16)


def _out_proj_ln(attn, w, x, g, b, alpha, *, tm=512):
    m, k = attn.shape
    d = w.shape[1]
    row = pl.BlockSpec((tm, d), lambda i: (i, 0))
    vec = pl.BlockSpec((1, d), lambda i: (0, 0))
    return pl.pallas_call(
        functools.partial(_out_ln_kernel, alpha=alpha),
        out_shape=(jax.ShapeDtypeStruct((m, d), F32), jax.ShapeDtypeStruct((m, d), BF16)),
        grid=(m // tm,),
        in_specs=[pl.BlockSpec((tm, k), lambda i: (i, 0)),
                  pl.BlockSpec((k, d), lambda i: (0, 0)),
                  row, vec, vec],
        out_specs=(row, row),
        compiler_params=pltpu.CompilerParams(
            dimension_semantics=("parallel",),
            vmem_limit_bytes=V7X_VMEM_LIMIT_BYTES),
        name="out_proj_ln",
    )(attn, w, x, g.reshape(1, d), b.reshape(1, d))


def _mlp_ln_kernel(xb_ref, wu_ref, wd_ref, x_ref, g_ref, b_ref, of_ref, ob_ref, acc_ref, *, alpha):
    j = pl.program_id(1)

    @pl.when(j == 0)
    def _():
        acc_ref[...] = jnp.zeros_like(acc_ref)

    h = jnp.maximum(jnp.dot(xb_ref[...], wu_ref[...], preferred_element_type=F32), 0.0)
    acc_ref[...] += jnp.dot((h * h).astype(BF16), wd_ref[...], preferred_element_type=F32)

    @pl.when(j == pl.num_programs(1) - 1)
    def _():
        y = _layer_norm_rows(alpha * x_ref[...] + acc_ref[...], g_ref[...], b_ref[...])
        of_ref[...] = y
        ob_ref[...] = y.astype(BF16)


MLP_FF_TILE = 512


def _tile_up_weights(w_up_all):
    depth, d, f = w_up_all.shape
    return w_up_all.astype(BF16).reshape(depth, d, f // MLP_FF_TILE, MLP_FF_TILE).transpose(0, 2, 1, 3)


def _mlp_ln(xb, w_up_tiled, w_down_all, layer, x, g, b, alpha, *, tm=512):
    m, d = xb.shape
    _, n_tiles, _, tf = w_up_tiled.shape
    row = lambda: pl.BlockSpec((tm, d), lambda i, j: (i, 0))
    vec = pl.BlockSpec((1, d), lambda i, j: (0, 0))
    return pl.pallas_call(
        functools.partial(_mlp_ln_kernel, alpha=alpha),
        out_shape=(jax.ShapeDtypeStruct((m, d), F32), jax.ShapeDtypeStruct((m, d), BF16)),
        grid=(m // tm, n_tiles),
        in_specs=[row(),
                  pl.BlockSpec((None, None, d, tf), lambda i, j: (layer, j, 0, 0)),
                  pl.BlockSpec((None, tf, d), lambda i, j: (layer, j, 0)),
                  row(), vec, vec],
        out_specs=(row(), row()),
        scratch_shapes=[pltpu.VMEM((tm, d), F32)],
        compiler_params=pltpu.CompilerParams(
            dimension_semantics=("parallel", "arbitrary"),
            vmem_limit_bytes=V7X_VMEM_LIMIT_BYTES),
        name="mlp_ln",
    )(xb, w_up_tiled, w_down_all, x, g.reshape(1, d), b.reshape(1, d))


def _rope_tables(seq):
    half = ROT_DIM // 2
    inv_freq = ROPE_THETA ** (-jnp.arange(0, ROT_DIM, 2, dtype=F32) / ROT_DIM)
    ang = jnp.arange(seq, dtype=F32)[:, None] * inv_freq[None, :]
    cos, sin = jnp.cos(ang), jnp.sin(ang)
    zeros = lambda n: jnp.zeros((seq, n), F32)
    c = jnp.concatenate([cos, cos, jnp.ones((seq, HEAD_DIM - ROT_DIM), F32)], axis=1)
    sa = jnp.concatenate([-sin, zeros(HEAD_DIM - half)], axis=1)
    sb = jnp.concatenate([zeros(half), sin, zeros(HEAD_DIM - ROT_DIM)], axis=1)
    return c, sa, sb


def kernel(x, ln_g, ln_b, sb_w_qkv, sb_w_o, kv_w, diff_w_q, diff_lambda, diff_subln_g,
           diff_w_o, mlp_w_up, mlp_w_down):
    batch, seq, d_model = x.shape
    depth = ln_g.shape[0]
    n_a = sb_w_qkv.shape[0]
    assert depth - n_a == 1
    alpha = (2 * depth) ** 0.25
    sb_heads = d_model // HEAD_DIM
    diff_heads = d_model // DIFF_V_DIM
    rope_tabs = _rope_tables(seq)

    xf = x.reshape(batch * seq, d_model)
    xb = xf
    kv_wb = kv_w.astype(BF16)
    w_up_tiled = _tile_up_weights(mlp_w_up)
    w_down_b = mlp_w_down.astype(BF16)
    for layer in range(depth):
        if layer < n_a:
            qkv = _project(xb, sb_w_qkv[layer].astype(BF16), rope_tabs, 0, d_model, seq)
            attn = _stick_breaking_attention(qkv, batch, seq, sb_heads)
            w_o = sb_w_o[layer]
        else:
            j = layer - n_a
            w_cat = jnp.concatenate([diff_w_q[j].astype(BF16), kv_wb], axis=1)
            qkv = _project(xb, w_cat, rope_tabs, 2 * d_model, d_model, seq)
            lambda_init = 0.8 - 0.6 * math.exp(-0.3 * layer)
            attn = _differential_attention(qkv, diff_lambda[j], diff_subln_g[j], batch, seq,
                                           diff_heads, lambda_init)
            w_o = diff_w_o[j]
        xf, xb = _out_proj_ln(attn, w_o.astype(BF16), xf, ln_g[layer, 0], ln_b[layer, 0], alpha)
        xf, xb = _mlp_ln(xb, w_up_tiled, w_down_b, layer, xf, ln_g[layer, 1], ln_b[layer, 1], alpha)
    return xf.reshape(batch, seq, d_model)
```

```python
import functools
import math

import jax
import jax.numpy as jnp
from jax import lax
from jax.experimental import pallas as pl
from jax.experimental.pallas import tpu as pltpu

HEAD_DIM = 128
DIFF_V_DIM = 256
ROT_DIM = HEAD_DIM // 4
ROPE_THETA = 500000.0
LN_EPS = 1e-5
SUBLN_EPS = 1e-5
LOGIT_SCALE_LOG2 = HEAD_DIM ** -0.5 * math.log2(math.e)
MASKED_LOGIT = -1e30
SHIFT_SLACK_LOG2 = 100.0
EXP2_ARG_MAX = 126.0

V7X_VMEM_LIMIT_BYTES = 56 * 1024 * 1024

F32 = jnp.float32
BF16 = jnp.bfloat16


def _nt_dot(a, b):
    return lax.dot_general(a, b, (((1,), (1,)), ((), ())), preferred_element_type=F32)


def _layer_norm_rows(y, g, b):
    mu = jnp.mean(y, axis=-1, keepdims=True)
    yc = y - mu
    var = jnp.mean(yc * yc, axis=-1, keepdims=True)
    return yc * lax.rsqrt(var + LN_EPS) * g + b


EPILOGUE_ROWS = 256


def _proj_kernel(x_ref, w_ref, c_ref, sa_ref, sb_ref, o_ref, *, rope_col_blocks, q_col_blocks,
                 q_scale):
    j = pl.program_id(1)
    tm, tn = o_ref.shape
    scale = jnp.where(j < q_col_blocks, q_scale, 1.0)

    def chunks(epilogue):
        for r0 in range(0, tm, EPILOGUE_ROWS):
            rows = slice(r0, r0 + EPILOGUE_ROWS)
            acc = jnp.dot(x_ref[rows, :].astype(BF16), w_ref[...], preferred_element_type=F32)
            epilogue(rows, acc * scale)

    def plain(rows, acc):
        o_ref[rows, :] = acc.astype(o_ref.dtype)

    def rotary(rows, acc):
        c = c_ref[rows, :]
        sa = sa_ref[rows, :]
        sb = sb_ref[rows, :]
        for g in range(tn // HEAD_DIM):
            t = acc[:, g * HEAD_DIM:(g + 1) * HEAD_DIM]
            rot = (t * c + pltpu.roll(t, HEAD_DIM - ROT_DIM // 2, 1) * sa
                   + pltpu.roll(t, ROT_DIM // 2, 1) * sb)
            o_ref[rows, g * HEAD_DIM:(g + 1) * HEAD_DIM] = rot.astype(o_ref.dtype)

    if rope_col_blocks == 0:
        chunks(plain)
        return

    @pl.when(j < rope_col_blocks)
    def _():
        chunks(rotary)

    @pl.when(j >= rope_col_blocks)
    def _():
        chunks(plain)


def _project(x2d, w, rope_tabs, rope_cols, q_cols, seq, *, tm=1024, tn=1024):
    m, k = x2d.shape
    n = w.shape[1]
    assert m % tm == 0 and n % tn == 0 and seq % tm == 0
    assert rope_cols % tn == 0 and q_cols % tn == 0
    row_blocks_per_seq = seq // tm
    tab_spec = pl.BlockSpec((tm, HEAD_DIM), lambda i, j: (i % row_blocks_per_seq, 0))
    return pl.pallas_call(
        functools.partial(_proj_kernel, rope_col_blocks=rope_cols // tn,
                          q_col_blocks=q_cols // tn, q_scale=LOGIT_SCALE_LOG2),
        out_shape=jax.ShapeDtypeStruct((m, n), BF16),
        grid=(m // tm, n // tn),
        in_specs=[
            pl.BlockSpec((tm, k), lambda i, j: (i, 0)),
            pl.BlockSpec((k, tn), lambda i, j: (0, j)),
            tab_spec, tab_spec, tab_spec,
        ],
        out_specs=pl.BlockSpec((tm, tn), lambda i, j: (i, j)),
        compiler_params=pltpu.CompilerParams(
            dimension_semantics=("parallel", "parallel"),
            vmem_limit_bytes=V7X_VMEM_LIMIT_BYTES),
        name="proj",
    )(x2d, w, *rope_tabs)


def _on_top_rows(fn, x, tri):
    n = tri.shape[0]
    top = fn(x[:n], tri)
    return top if x.shape[0] == n else jnp.concatenate([top, x[n:]], axis=0)


def _sb_logits_stage(q, k, tri):
    z = _nt_dot(q, k)
    sp = jnp.maximum(z, jnp.log2(1.0 + jnp.exp2(jnp.minimum(z, EXP2_ARG_MAX))))
    ls = z - sp
    if tri is not None:
        sp = _on_top_rows(jnp.multiply, sp, tri)
    return ls, sp.astype(BF16), sp[:, :HEAD_DIM]


def _sb_weights_stage(ls, spb, sp0, v, u, acc_ref, carry_ref, r0, tri):
    groups = ls.shape[1] // HEAD_DIM
    excl = jnp.dot(spb, u, preferred_element_type=F32)
    carry = carry_ref[r0:, :]
    a = jnp.exp2(ls - jnp.concatenate([carry] * groups, axis=1) - excl)
    if tri is not None:
        a = _on_top_rows(jnp.multiply, a, tri)
    acc_ref[r0:, :] += jnp.dot(a.astype(BF16), v, preferred_element_type=F32)
    carry_ref[r0:, :] = carry + (excl[:, :1] + sp0[:, :1])


def _sb_kernel(q_ref, k_ref, v_ref, u_ref, o_ref, acc_ref, carry_ref, ls_ref, spb_ref, sp0_ref,
               *, blk, chunks):
    assert chunks % 2 == 0
    qi = pl.program_id(2)
    u = u_ref[...]
    base = qi * chunks

    def k_tile(t):
        return k_ref[pl.ds(pl.multiple_of(t * blk, blk), blk), :]

    def v_tile(t):
        return v_ref[pl.ds(pl.multiple_of(t * blk, blk), blk), :]

    acc_ref[...] = jnp.zeros(acc_ref.shape, F32)
    carry_ref[...] = jnp.zeros(carry_ref.shape, F32)

    tri = u.astype(F32)
    for t in reversed(range(chunks)):
        r0 = t * blk
        ls, spb, sp0 = _sb_logits_stage(q_ref[r0:, :], k_tile(base + t), tri)
        _sb_weights_stage(ls, spb, sp0, v_tile(base + t), u, acc_ref, carry_ref, r0, tri)

    def logits_to_slot(t, slot):
        ls_ref[slot], spb_ref[slot], sp0_ref[slot] = _sb_logits_stage(q_ref[...], k_tile(t), None)

    def weights_from_slot(t, slot):
        _sb_weights_stage(ls_ref[slot], spb_ref[slot], sp0_ref[slot], v_tile(t), u,
                          acc_ref, carry_ref, 0, None)

    @pl.when(base > 0)
    def _():
        logits_to_slot(base - 1, 0)

        def group(first, is_last):
            for j in range(chunks):
                weights_from_slot(first - j, j % 2)
                if not (is_last and j == chunks - 1):
                    logits_to_slot(first - j - 1, (j + 1) % 2)

        def body(i, c):
            group(base - 1 - chunks * i, False)
            return c

        lax.fori_loop(0, qi - 1, body, 0)
        group(chunks - 1, True)

    o_ref[...] = acc_ref[...].astype(o_ref.dtype)


def _stick_breaking_attention(qkv, batch, seq, heads, *, blk=256, chunks=4):
    m = qkv.shape[0]
    bq = blk * chunks
    nq = seq // bq
    idx = jnp.arange(blk)
    u = (idx[:, None] > idx[None, :]).astype(BF16)
    return pl.pallas_call(
        functools.partial(_sb_kernel, blk=blk, chunks=chunks),
        out_shape=jax.ShapeDtypeStruct((m, heads * HEAD_DIM), BF16),
        grid=(batch, heads, nq),
        in_specs=[
            pl.BlockSpec((bq, HEAD_DIM), lambda b, h, i: (b * nq + i, h)),
            pl.BlockSpec((seq, HEAD_DIM), lambda b, h, i: (b, heads + h)),
            pl.BlockSpec((seq, HEAD_DIM), lambda b, h, i: (b, 2 * heads + h)),
            pl.BlockSpec((blk, blk), lambda b, h, i: (0, 0)),
        ],
        out_specs=pl.BlockSpec((bq, HEAD_DIM), lambda b, h, i: (b * nq + i, h)),
        scratch_shapes=[pltpu.VMEM((bq, HEAD_DIM), F32),
                        pltpu.VMEM((bq, HEAD_DIM), F32),
                        pltpu.VMEM((2, bq, blk), F32),
                        pltpu.VMEM((2, bq, blk), BF16),
                        pltpu.VMEM((2, bq, HEAD_DIM), F32)],
        compiler_params=pltpu.CompilerParams(
            dimension_semantics=("parallel", "parallel", "parallel"),
            vmem_limit_bytes=V7X_VMEM_LIMIT_BYTES),
        name="sb_attn",
    )(qkv, qkv, qkv, u)


def _diff_kernel(q_ref, k_ref, v_ref, lam_ref, g_ref, o_ref, m_ref, l_ref, acc_ref, kmax_ref, *,
                 blk, chunks, lambda_init):
    qi = pl.program_id(2)
    bq = blk * chunks
    wide = 2 * blk
    assert chunks % 2 == 0

    def scores(c, r0, k0, width, masked):
        lanes = slice(c * HEAD_DIM, (c + 1) * HEAD_DIM)
        s = _nt_dot(q_ref[r0:, lanes], k_ref[pl.ds(k0, width), lanes])
        if masked:
            s = _on_top_rows(jnp.add, s, band_bias)
        return [s[:, g * HEAD_DIM:(g + 1) * HEAD_DIM] for g in range(width // HEAD_DIM)]

    row = lax.broadcasted_iota(jnp.int32, (blk, blk), 0)
    col = lax.broadcasted_iota(jnp.int32, (blk, blk), 1)
    band_bias = jnp.where(col <= row, 0.0, MASKED_LOGIT)

    def sweep(tile_fn):
        for t in range(chunks):
            tile_fn(t * blk, pl.multiple_of((qi * chunks + t) * blk, blk), blk, True)

        def body(i, carry):
            for j in range(chunks // 2):
                tile_fn(0, pl.multiple_of((i * (chunks // 2) + j) * wide, wide), wide, False)
            return carry

        lax.fori_loop(0, qi, body, 0)

    def max_tile(r0, k0, width, masked):
        for c in range(2):
            m_ref[c, r0:, :] = functools.reduce(
                jnp.maximum, scores(c, r0, k0, width, masked), m_ref[c, r0:, :])

    def pv_tile(r0, k0, width, masked):
        v = v_ref[pl.ds(k0, width), :]
        for c in range(2):
            m = m_ref[c, r0:, :]
            p = [jnp.exp2(s - m) for s in scores(c, r0, k0, width, masked)]
            l_ref[c, r0:, :] += functools.reduce(jnp.add, p)
            acc_ref[c, r0:, :] += jnp.dot(jnp.concatenate(p, axis=1).astype(BF16), v,
                                          preferred_element_type=F32)

    @pl.when(qi == 0)
    def _():
        def body(i, best):
            kf = k_ref[pl.ds(pl.multiple_of(i * wide, wide), wide), :].astype(F32)
            sq = kf * kf
            return tuple(
                jnp.maximum(best[c], jnp.sum(sq[:, c * HEAD_DIM:(c + 1) * HEAD_DIM], axis=1,
                                             keepdims=True)) for c in range(2))

        best = lax.fori_loop(0, k_ref.shape[0] // wide, body,
                             (jnp.zeros((wide, 1), F32),) * 2)
        for c in range(2):
            kmax_ref[c] = jnp.broadcast_to(jnp.max(best[c], axis=0, keepdims=True),
                                           kmax_ref.shape[1:])

    qf = q_ref[...].astype(F32)
    worst = []
    for c in range(2):
        lanes = slice(c * HEAD_DIM, (c + 1) * HEAD_DIM)
        q_sq = jnp.sum(qf[:, lanes] * qf[:, lanes], axis=1, keepdims=True)
        bound = jnp.sqrt(q_sq * kmax_ref[c, :1, :1])
        m_ref[c] = jnp.broadcast_to(bound, m_ref.shape[1:])
        worst.append(jnp.max(bound))
    bound_ok = 2.0 * jnp.maximum(worst[0], worst[1]) <= SHIFT_SLACK_LOG2

    @pl.when(jnp.logical_not(bound_ok))
    def _():
        m_ref[...] = jnp.full(m_ref.shape, MASKED_LOGIT, F32)
        sweep(max_tile)
        for c in range(2):
            m_ref[c] = jnp.broadcast_to(jnp.max(m_ref[c], axis=1, keepdims=True),
                                        m_ref.shape[1:])

    l_ref[...] = jnp.zeros(l_ref.shape, F32)
    acc_ref[...] = jnp.zeros(acc_ref.shape, F32)
    sweep(pv_tile)

    lam = lam_ref[...]
    lam_full = (jnp.exp(jnp.sum(lam[0:1] * lam[1:2], axis=1, keepdims=True))
                - jnp.exp(jnp.sum(lam[2:3] * lam[3:4], axis=1, keepdims=True)) + lambda_init)
    inv_l = [1.0 / jnp.sum(l_ref[c], axis=1, keepdims=True) for c in range(2)]
    o = acc_ref[0] * inv_l[0] - lam_full * (acc_ref[1] * inv_l[1])
    o = o * lax.rsqrt(jnp.mean(o * o, axis=-1, keepdims=True) + SUBLN_EPS)
    o_ref[...] = (o * (g_ref[...] * (1.0 - lambda_init))).astype(o_ref.dtype)


def _differential_attention(qkv, lam, subln_g, batch, seq, heads, lambda_init, *, blk=256,
                            chunks=4):
    m = qkv.shape[0]
    bq = blk * chunks
    nq = seq // bq
    w = DIFF_V_DIM
    return pl.pallas_call(
        functools.partial(_diff_kernel, blk=blk, chunks=chunks, lambda_init=lambda_init),
        out_shape=jax.ShapeDtypeStruct((m, heads * w), BF16),
        grid=(batch, heads, nq),
        in_specs=[
            pl.BlockSpec((bq, w), lambda b, h, i: (b * nq + i, h)),
            pl.BlockSpec((seq, w), lambda b, h, i: (b, heads + h)),
            pl.BlockSpec((seq, w), lambda b, h, i: (b, 2 * heads + h)),
            pl.BlockSpec((4, HEAD_DIM), lambda b, h, i: (0, 0)),
            pl.BlockSpec((1, w), lambda b, h, i: (0, 0)),
        ],
        out_specs=pl.BlockSpec((bq, w), lambda b, h, i: (b * nq + i, h)),
        scratch_shapes=[pltpu.VMEM((2, bq, HEAD_DIM), F32), pltpu.VMEM((2, bq, HEAD_DIM), F32),
                        pltpu.VMEM((2, bq, w), F32), pltpu.VMEM((2, 8, HEAD_DIM), F32)],
        compiler_params=pltpu.CompilerParams(
            dimension_semantics=("parallel", "parallel", "arbitrary"),
            vmem_limit_bytes=V7X_VMEM_LIMIT_BYTES),
        name="diff_attn",
    )(qkv, qkv, qkv, lam, subln_g.reshape(1, w))


def _out_ln_kernel(a_ref, w_ref, x_ref, g_ref, b_ref, of_ref, ob_ref, *, alpha):
    for r0 in range(0, of_ref.shape[0], EPILOGUE_ROWS):
        rows = slice(r0, r0 + EPILOGUE_ROWS)
        h = jnp.dot(a_ref[rows, :], w_ref[...], preferred_element_type=F32)
        y = _layer_norm_rows(alpha * x_ref[rows, :] + h, g_ref[...], b_ref[...])
        of_ref[rows, :] = y
        ob_ref[rows, :] = y.astype(BF16)


def _out_proj_ln(attn, w, x, g, b, alpha, *, tm=512):
    m, k = attn.shape
    d = w.shape[1]
    row = pl.BlockSpec((tm, d), lambda i: (i, 0))
    vec = pl.BlockSpec((1, d), lambda i: (0, 0))
    return pl.pallas_call(
        functools.partial(_out_ln_kernel, alpha=alpha),
        out_shape=(jax.ShapeDtypeStruct((m, d), F32), jax.ShapeDtypeStruct((m, d), BF16)),
        grid=(m // tm,),
        in_specs=[pl.BlockSpec((tm, k), lambda i: (i, 0)),
                  pl.BlockSpec((k, d), lambda i: (0, 0)),
                  row, vec, vec],
        out_specs=(row, row),
        compiler_params=pltpu.CompilerParams(
            dimension_semantics=("parallel",),
            vmem_limit_bytes=V7X_VMEM_LIMIT_BYTES),
        name="out_proj_ln",
    )(attn, w, x, g.reshape(1, d), b.reshape(1, d))


def _mlp_ln_kernel(xb_ref, wu_ref, wd_ref, x_ref, g_ref, b_ref, of_ref, ob_ref, acc_ref, *, alpha):
    j = pl.program_id(1)
    last = pl.num_programs(1) - 1

    def partial_out(rows):
        h = jnp.maximum(jnp.dot(xb_ref[rows, :], wu_ref[...], preferred_element_type=F32), 0.0)
        return jnp.dot((h * h).astype(BF16), wd_ref[...], preferred_element_type=F32)

    @pl.when(j == 0)
    def _():
        acc_ref[...] = partial_out(slice(None))

    @pl.when(jnp.logical_and(j > 0, j < last))
    def _():
        acc_ref[...] += partial_out(slice(None))

    @pl.when(j == last)
    def _():
        for r0 in range(0, of_ref.shape[0], EPILOGUE_ROWS):
            rows = slice(r0, r0 + EPILOGUE_ROWS)
            y = _layer_norm_rows(alpha * x_ref[rows, :] + (acc_ref[rows, :] + partial_out(rows)),
                                 g_ref[...], b_ref[...])
            of_ref[rows, :] = y
            ob_ref[rows, :] = y.astype(BF16)


MLP_FF_TILE = 512


def _tile_up_weights(w_up_all):
    depth, d, f = w_up_all.shape
    return w_up_all.astype(BF16).reshape(depth, d, f // MLP_FF_TILE, MLP_FF_TILE).transpose(0, 2, 1, 3)


def _mlp_ln(xb, w_up_tiled, w_down_all, layer, x, g, b, alpha, *, tm=512):
    m, d = xb.shape
    _, n_tiles, _, tf = w_up_tiled.shape
    assert n_tiles >= 2
    row = lambda: pl.BlockSpec((tm, d), lambda i, j: (i, 0))
    vec = pl.BlockSpec((1, d), lambda i, j: (0, 0))
    return pl.pallas_call(
        functools.partial(_mlp_ln_kernel, alpha=alpha),
        out_shape=(jax.ShapeDtypeStruct((m, d), F32), jax.ShapeDtypeStruct((m, d), BF16)),
        grid=(m // tm, n_tiles),
        in_specs=[row(),
                  pl.BlockSpec((None, None, d, tf), lambda i, j: (layer, j, 0, 0)),
                  pl.BlockSpec((None, tf, d), lambda i, j: (layer, j, 0)),
                  row(), vec, vec],
        out_specs=(row(), row()),
        scratch_shapes=[pltpu.VMEM((tm, d), F32)],
        compiler_params=pltpu.CompilerParams(
            dimension_semantics=("parallel", "arbitrary"),
            vmem_limit_bytes=V7X_VMEM_LIMIT_BYTES),
        name="mlp_ln",
    )(xb, w_up_tiled, w_down_all, x, g.reshape(1, d), b.reshape(1, d))


def _rope_tables(seq):
    half = ROT_DIM // 2
    inv_freq = ROPE_THETA ** (-jnp.arange(0, ROT_DIM, 2, dtype=F32) / ROT_DIM)
    ang = jnp.arange(seq, dtype=F32)[:, None] * inv_freq[None, :]
    cos, sin = jnp.cos(ang), jnp.sin(ang)
    zeros = lambda n: jnp.zeros((seq, n), F32)
    c = jnp.concatenate([cos, cos, jnp.ones((seq, HEAD_DIM - ROT_DIM), F32)], axis=1)
    sa = jnp.concatenate([-sin, zeros(HEAD_DIM - half)], axis=1)
    sb = jnp.concatenate([zeros(half), sin, zeros(HEAD_DIM - ROT_DIM)], axis=1)
    return c, sa, sb


def kernel(x, ln_g, ln_b, sb_w_qkv, sb_w_o, kv_w, diff_w_q, diff_lambda, diff_subln_g,
           diff_w_o, mlp_w_up, mlp_w_down):
    batch, seq, d_model = x.shape
    depth = ln_g.shape[0]
    n_a = sb_w_qkv.shape[0]
    assert depth - n_a == 1
    alpha = (2 * depth) ** 0.25
    sb_heads = d_model // HEAD_DIM
    diff_heads = d_model // DIFF_V_DIM
    rope_tabs = _rope_tables(seq)

    xf = x.reshape(batch * seq, d_model)
    xb = xf
    kv_wb = kv_w.astype(BF16)
    w_up_tiled = _tile_up_weights(mlp_w_up)
    w_down_b = mlp_w_down.astype(BF16)
    for layer in range(depth):
        if layer < n_a:
            qkv = _project(xb, sb_w_qkv[layer].astype(BF16), rope_tabs, 0, d_model, seq)
            attn = _stick_breaking_attention(qkv, batch, seq, sb_heads)
            w_o = sb_w_o[layer]
        else:
            j = layer - n_a
            w_cat = jnp.concatenate([diff_w_q[j].astype(BF16), kv_wb], axis=1)
            qkv = _project(xb, w_cat, rope_tabs, 2 * d_model, d_model, seq)
            lambda_init = 0.8 - 0.6 * math.exp(-0.3 * layer)
            attn = _differential_attention(qkv, diff_lambda[j], diff_subln_g[j], batch, seq,
                                           diff_heads, lambda_init)
            w_o = diff_w_o[j]
        xf, xb = _out_proj_ln(attn, w_o.astype(BF16), xf, ln_g[layer, 0], ln_b[layer, 0], alpha)
        xf, xb = _mlp_ln(xb, w_up_tiled, w_down_b, layer, xf, ln_g[layer, 1], ln_b[layer, 1], alpha)
    return xf.reshape(batch, seq, d_model)
```

```python
import functools
import math

import jax
import jax.numpy as jnp
from jax import lax
from jax.experimental import pallas as pl
from jax.experimental.pallas import tpu as pltpu

HEAD_DIM = 128
DIFF_V_DIM = 256
ROT_DIM = HEAD_DIM // 4
ROPE_THETA = 500000.0
LN_EPS = 1e-5
SUBLN_EPS = 1e-5
LOGIT_SCALE_LOG2 = HEAD_DIM ** -0.5 * math.log2(math.e)
MASKED_LOGIT = -1e30
SHIFT_SLACK_LOG2 = 100.0
EXP2_ARG_MAX = 126.0

V7X_VMEM_LIMIT_BYTES = 56 * 1024 * 1024

F32 = jnp.float32
BF16 = jnp.bfloat16


def _nt_dot(a, b):
    return lax.dot_general(a, b, (((1,), (1,)), ((), ())), preferred_element_type=F32)


def _layer_norm_rows(y, g, b):
    mu = jnp.mean(y, axis=-1, keepdims=True)
    yc = y - mu
    var = jnp.mean(yc * yc, axis=-1, keepdims=True)
    return yc * lax.rsqrt(var + LN_EPS) * g + b


EPILOGUE_ROWS = 256


def _proj_kernel(x_ref, w_ref, c_ref, sa_ref, sb_ref, o_ref, *, rope_col_blocks, q_col_blocks,
                 q_scale):
    j = pl.program_id(1)
    tm, tn = o_ref.shape
    scale = jnp.where(j < q_col_blocks, q_scale, 1.0)

    def chunks(epilogue):
        for r0 in range(0, tm, EPILOGUE_ROWS):
            rows = slice(r0, r0 + EPILOGUE_ROWS)
            acc = jnp.dot(x_ref[rows, :].astype(BF16), w_ref[...], preferred_element_type=F32)
            epilogue(rows, acc * scale)

    def plain(rows, acc):
        o_ref[rows, :] = acc.astype(o_ref.dtype)

    def rotary(rows, acc):
        c = c_ref[rows, :]
        sa = sa_ref[rows, :]
        sb = sb_ref[rows, :]
        for g in range(tn // HEAD_DIM):
            t = acc[:, g * HEAD_DIM:(g + 1) * HEAD_DIM]
            rot = (t * c + pltpu.roll(t, HEAD_DIM - ROT_DIM // 2, 1) * sa
                   + pltpu.roll(t, ROT_DIM // 2, 1) * sb)
            o_ref[rows, g * HEAD_DIM:(g + 1) * HEAD_DIM] = rot.astype(o_ref.dtype)

    if rope_col_blocks == 0:
        chunks(plain)
        return

    @pl.when(j < rope_col_blocks)
    def _():
        chunks(rotary)

    @pl.when(j >= rope_col_blocks)
    def _():
        chunks(plain)


def _project(x2d, w, rope_tabs, rope_cols, q_cols, seq, *, tm=1024, tn=1024):
    m, k = x2d.shape
    n = w.shape[1]
    assert m % tm == 0 and n % tn == 0 and seq % tm == 0
    assert rope_cols % tn == 0 and q_cols % tn == 0
    row_blocks_per_seq = seq // tm
    tab_spec = pl.BlockSpec((tm, HEAD_DIM), lambda i, j: (i % row_blocks_per_seq, 0))
    return pl.pallas_call(
        functools.partial(_proj_kernel, rope_col_blocks=rope_cols // tn,
                          q_col_blocks=q_cols // tn, q_scale=LOGIT_SCALE_LOG2),
        out_shape=jax.ShapeDtypeStruct((m, n), BF16),
        grid=(m // tm, n // tn),
        in_specs=[
            pl.BlockSpec((tm, k), lambda i, j: (i, 0)),
            pl.BlockSpec((k, tn), lambda i, j: (0, j)),
            tab_spec, tab_spec, tab_spec,
        ],
        out_specs=pl.BlockSpec((tm, tn), lambda i, j: (i, j)),
        compiler_params=pltpu.CompilerParams(
            dimension_semantics=("parallel", "parallel"),
            vmem_limit_bytes=V7X_VMEM_LIMIT_BYTES),
        name="proj",
    )(x2d, w, *rope_tabs)


def _sb_logits_stage(q, k, strict):
    z = _nt_dot(q, k)
    sp = jnp.maximum(z, jnp.log2(1.0 + jnp.exp2(jnp.minimum(z, EXP2_ARG_MAX))))
    ls = z - sp
    if strict is not None:
        sp = jnp.where(strict, sp, 0.0)
    return ls, sp.astype(BF16), sp[:, :HEAD_DIM]


def _sb_weights_stage(ls, spb, sp0, v, u, acc_ref, carry_ref, r0, strict):
    groups = ls.shape[1] // HEAD_DIM
    excl = jnp.dot(spb, u, preferred_element_type=F32)
    carry = carry_ref[r0:, :]
    a = jnp.exp2(ls - jnp.concatenate([carry] * groups, axis=1) - excl)
    if strict is not None:
        a = jnp.where(strict, a, 0.0)
    acc_ref[r0:, :] += jnp.dot(a.astype(BF16), v, preferred_element_type=F32)
    carry_ref[r0:, :] = carry + (excl[:, :1] + sp0[:, :1])


def _sb_kernel(q_ref, k_ref, v_ref, u_ref, o_ref, acc_ref, carry_ref, ls_ref, spb_ref, sp0_ref,
               *, blk, chunks):
    assert chunks % 2 == 0
    qi = pl.program_id(2)
    u = u_ref[...]
    base = qi * chunks

    def k_tile(t):
        return k_ref[pl.ds(pl.multiple_of(t * blk, blk), blk), :]

    def v_tile(t):
        return v_ref[pl.ds(pl.multiple_of(t * blk, blk), blk), :]

    acc_ref[...] = jnp.zeros(acc_ref.shape, F32)
    carry_ref[...] = jnp.zeros(carry_ref.shape, F32)

    for t in reversed(range(chunks)):
        r0 = t * blk
        row = lax.broadcasted_iota(jnp.int32, (blk * chunks - r0, blk), 0)
        col = lax.broadcasted_iota(jnp.int32, (blk * chunks - r0, blk), 1)
        strict = col < row
        ls, spb, sp0 = _sb_logits_stage(q_ref[r0:, :], k_tile(base + t), strict)
        _sb_weights_stage(ls, spb, sp0, v_tile(base + t), u, acc_ref, carry_ref, r0, strict)

    def logits_to_slot(t, slot):
        ls_ref[slot], spb_ref[slot], sp0_ref[slot] = _sb_logits_stage(q_ref[...], k_tile(t), None)

    def weights_from_slot(t, slot):
        _sb_weights_stage(ls_ref[slot], spb_ref[slot], sp0_ref[slot], v_tile(t), u,
                          acc_ref, carry_ref, 0, None)

    logits_to_slot(jnp.maximum(base - 1, 0), 0)

    @pl.when(base > 0)
    def _():
        def group(first, is_last):
            for j in range(chunks):
                weights_from_slot(first - j, j % 2)
                if not (is_last and j == chunks - 1):
                    logits_to_slot(first - j - 1, (j + 1) % 2)

        def body(i, c):
            group(base - 1 - chunks * i, False)
            return c

        lax.fori_loop(0, qi - 1, body, 0)
        group(chunks - 1, True)

    o_ref[...] = acc_ref[...].astype(o_ref.dtype)


def _stick_breaking_attention(qkv, batch, seq, heads, *, blk=256, chunks=4):
    m = qkv.shape[0]
    bq = blk * chunks
    nq = seq // bq
    idx = jnp.arange(blk)
    u = (idx[:, None] > idx[None, :]).astype(BF16)
    return pl.pallas_call(
        functools.partial(_sb_kernel, blk=blk, chunks=chunks),
        out_shape=jax.ShapeDtypeStruct((m, heads * HEAD_DIM), BF16),
        grid=(batch, heads, nq),
        in_specs=[
            pl.BlockSpec((bq, HEAD_DIM), lambda b, h, i: (b * nq + i, h)),
            pl.BlockSpec((seq, HEAD_DIM), lambda b, h, i: (b, heads + h)),
            pl.BlockSpec((seq, HEAD_DIM), lambda b, h, i: (b, 2 * heads + h)),
            pl.BlockSpec((blk, blk), lambda b, h, i: (0, 0)),
        ],
        out_specs=pl.BlockSpec((bq, HEAD_DIM), lambda b, h, i: (b * nq + i, h)),
        scratch_shapes=[pltpu.VMEM((bq, HEAD_DIM), F32),
                        pltpu.VMEM((bq, HEAD_DIM), F32),
                        pltpu.VMEM((2, bq, blk), F32),
                        pltpu.VMEM((2, bq, blk), BF16),
                        pltpu.VMEM((2, bq, HEAD_DIM), F32)],
        compiler_params=pltpu.CompilerParams(
            dimension_semantics=("parallel", "parallel", "parallel"),
            vmem_limit_bytes=V7X_VMEM_LIMIT_BYTES),
        name="sb_attn",
    )(qkv, qkv, qkv, u)


def _diff_kernel(q_ref, k_ref, v_ref, lam_ref, g_ref, o_ref, m_ref, l_ref, acc_ref, kmax_ref, *,
                 blk, chunks, lambda_init):
    qi = pl.program_id(2)
    bq = blk * chunks
    wide = 2 * blk
    assert chunks % 2 == 0

    def scores(c, r0, k0, width, masked):
        lanes = slice(c * HEAD_DIM, (c + 1) * HEAD_DIM)
        s = _nt_dot(q_ref[r0:, lanes], k_ref[pl.ds(k0, width), lanes])
        if masked:
            row = lax.broadcasted_iota(jnp.int32, s.shape, 0)
            col = lax.broadcasted_iota(jnp.int32, s.shape, 1)
            s = jnp.where(col <= row, s, MASKED_LOGIT)
        return [s[:, g * HEAD_DIM:(g + 1) * HEAD_DIM] for g in range(width // HEAD_DIM)]

    def sweep(tile_fn):
        for t in range(chunks):
            tile_fn(t * blk, pl.multiple_of((qi * chunks + t) * blk, blk), blk, True)

        def body(i, carry):
            for j in range(chunks // 2):
                tile_fn(0, pl.multiple_of((i * (chunks // 2) + j) * wide, wide), wide, False)
            return carry

        lax.fori_loop(0, qi, body, 0)

    def max_tile(r0, k0, width, masked):
        for c in range(2):
            m_ref[c, r0:, :] = functools.reduce(
                jnp.maximum, scores(c, r0, k0, width, masked), m_ref[c, r0:, :])

    def pv_tile(r0, k0, width, masked):
        v = v_ref[pl.ds(k0, width), :]
        for c in range(2):
            m = m_ref[c, r0:, :]
            p = [jnp.exp2(s - m) for s in scores(c, r0, k0, width, masked)]
            l_ref[c, r0:, :] += functools.reduce(jnp.add, p)
            acc_ref[c, r0:, :] += jnp.dot(jnp.concatenate(p, axis=1).astype(BF16), v,
                                          preferred_element_type=F32)

    half_r = lax.broadcasted_iota(jnp.int32, (2 * HEAD_DIM, 2 * HEAD_DIM), 0) >= HEAD_DIM
    half_c = lax.broadcasted_iota(jnp.int32, (2 * HEAD_DIM, 2 * HEAD_DIM), 1) >= HEAD_DIM
    ones_bd = jnp.where(half_r == half_c, 1.0, 0.0).astype(BF16)

    def sq_norms(x):
        return jnp.dot(x * x, ones_bd, preferred_element_type=F32)

    @pl.when(qi == 0)
    def _():
        def body(i, best):
            return jnp.maximum(
                best, sq_norms(k_ref[pl.ds(pl.multiple_of(i * wide, wide), wide), :]))

        best = lax.fori_loop(0, k_ref.shape[0] // wide, body,
                             jnp.zeros((wide, 2 * HEAD_DIM), F32))
        kmax_ref[...] = jnp.broadcast_to(jnp.max(best, axis=0, keepdims=True), kmax_ref.shape)

    bound = jnp.sqrt(sq_norms(q_ref[...]) * kmax_ref[:1, :])
    for c in range(2):
        m_ref[c] = bound[:, c * HEAD_DIM:(c + 1) * HEAD_DIM]
    bound_ok = 2.0 * jnp.max(bound) <= SHIFT_SLACK_LOG2

    @pl.when(jnp.logical_not(bound_ok))
    def _():
        m_ref[...] = jnp.full(m_ref.shape, MASKED_LOGIT, F32)
        sweep(max_tile)
        for c in range(2):
            m_ref[c] = jnp.broadcast_to(jnp.max(m_ref[c], axis=1, keepdims=True),
                                        m_ref.shape[1:])

    l_ref[...] = jnp.zeros(l_ref.shape, F32)
    acc_ref[...] = jnp.zeros(acc_ref.shape, F32)
    sweep(pv_tile)

    lam = lam_ref[...]
    lam_full = (jnp.exp(jnp.sum(lam[0:1] * lam[1:2], axis=1, keepdims=True))
                - jnp.exp(jnp.sum(lam[2:3] * lam[3:4], axis=1, keepdims=True)) + lambda_init)
    inv_l = [1.0 / jnp.sum(l_ref[c], axis=1, keepdims=True) for c in range(2)]
    o = acc_ref[0] * inv_l[0] - lam_full * (acc_ref[1] * inv_l[1])
    o = o * lax.rsqrt(jnp.mean(o * o, axis=-1, keepdims=True) + SUBLN_EPS)
    o_ref[...] = (o * (g_ref[...] * (1.0 - lambda_init))).astype(o_ref.dtype)


def _differential_attention(qkv, lam, subln_g, batch, seq, heads, lambda_init, *, blk=256,
                            chunks=4):
    m = qkv.shape[0]
    bq = blk * chunks
    nq = seq // bq
    w = DIFF_V_DIM
    return pl.pallas_call(
        functools.partial(_diff_kernel, blk=blk, chunks=chunks, lambda_init=lambda_init),
        out_shape=jax.ShapeDtypeStruct((m, heads * w), BF16),
        grid=(batch, heads, nq),
        in_specs=[
            pl.BlockSpec((bq, w), lambda b, h, i: (b * nq + i, h)),
            pl.BlockSpec((seq, w), lambda b, h, i: (b, heads + h)),
            pl.BlockSpec((seq, w), lambda b, h, i: (b, 2 * heads + h)),
            pl.BlockSpec((4, HEAD_DIM), lambda b, h, i: (0, 0)),
            pl.BlockSpec((1, w), lambda b, h, i: (0, 0)),
        ],
        out_specs=pl.BlockSpec((bq, w), lambda b, h, i: (b * nq + i, h)),
        scratch_shapes=[pltpu.VMEM((2, bq, HEAD_DIM), F32), pltpu.VMEM((2, bq, HEAD_DIM), F32),
                        pltpu.VMEM((2, bq, w), F32), pltpu.VMEM((8, 2 * HEAD_DIM), F32)],
        compiler_params=pltpu.CompilerParams(
            dimension_semantics=("parallel", "parallel", "arbitrary"),
            vmem_limit_bytes=V7X_VMEM_LIMIT_BYTES),
        name="diff_attn",
    )(qkv, qkv, qkv, lam, subln_g.reshape(1, w))


def _out_ln_kernel(a_ref, w_ref, x_ref, g_ref, b_ref, of_ref, ob_ref, *, alpha):
    for r0 in range(0, of_ref.shape[0], EPILOGUE_ROWS):
        rows = slice(r0, r0 + EPILOGUE_ROWS)
        h = jnp.dot(a_ref[rows, :], w_ref[...], preferred_element_type=F32)
        y = _layer_norm_rows(alpha * x_ref[rows, :] + h, g_ref[...], b_ref[...])
        of_ref[rows, :] = y
        ob_ref[rows, :] = y.astype(BF16)


def _out_proj_ln(attn, w, x, g, b, alpha, *, tm=512):
    m, k = attn.shape
    d = w.shape[1]
    row = pl.BlockSpec((tm, d), lambda i: (i, 0))
    vec = pl.BlockSpec((1, d), lambda i: (0, 0))
    return pl.pallas_call(
        functools.partial(_out_ln_kernel, alpha=alpha),
        out_shape=(jax.ShapeDtypeStruct((m, d), F32), jax.ShapeDtypeStruct((m, d), BF16)),
        grid=(m // tm,),
        in_specs=[pl.BlockSpec((tm, k), lambda i: (i, 0)),
                  pl.BlockSpec((k, d), lambda i: (0, 0)),
                  row, vec, vec],
        out_specs=(row, row),
        compiler_params=pltpu.CompilerParams(
            dimension_semantics=("parallel",),
            vmem_limit_bytes=V7X_VMEM_LIMIT_BYTES),
        name="out_proj_ln",
    )(attn, w, x, g.reshape(1, d), b.reshape(1, d))


def _mlp_ln_kernel(xb_ref, wu_ref, wd_ref, x_ref, g_ref, b_ref, of_ref, ob_ref, acc_ref, *, alpha):
    j = pl.program_id(1)

    @pl.when(j == 0)
    def _():
        acc_ref[...] = jnp.zeros_like(acc_ref)

    h = jnp.maximum(jnp.dot(xb_ref[...], wu_ref[...], preferred_element_type=F32), 0.0)
    acc_ref[...] += jnp.dot((h * h).astype(BF16), wd_ref[...], preferred_element_type=F32)

    @pl.when(j == pl.num_programs(1) - 1)
    def _():
        y = _layer_norm_rows(alpha * x_ref[...] + acc_ref[...], g_ref[...], b_ref[...])
        of_ref[...] = y
        ob_ref[...] = y.astype(BF16)


MLP_FF_TILE = 1024


def _mlp_ln(xb, w_up_all, w_down_all, layer, x, g, b, alpha, *, tm=512, tf=MLP_FF_TILE):
    m, d = xb.shape
    f = w_up_all.shape[2]
    row = lambda: pl.BlockSpec((tm, d), lambda i, j: (i, 0))
    vec = pl.BlockSpec((1, d), lambda i, j: (0, 0))
    return pl.pallas_call(
        functools.partial(_mlp_ln_kernel, alpha=alpha),
        out_shape=(jax.ShapeDtypeStruct((m, d), F32), jax.ShapeDtypeStruct((m, d), BF16)),
        grid=(m // tm, f // tf),
        in_specs=[row(),
                  pl.BlockSpec((None, d, tf), lambda i, j: (layer, 0, j)),
                  pl.BlockSpec((None, tf, d), lambda i, j: (layer, j, 0)),
                  row(), vec, vec],
        out_specs=(row(), row()),
        scratch_shapes=[pltpu.VMEM((tm, d), F32)],
        compiler_params=pltpu.CompilerParams(
            dimension_semantics=("parallel", "arbitrary"),
            vmem_limit_bytes=V7X_VMEM_LIMIT_BYTES),
        name="mlp_ln",
    )(xb, w_up_all, w_down_all, x, g.reshape(1, d), b.reshape(1, d))


def _rope_tables(seq):
    half = ROT_DIM // 2
    inv_freq = ROPE_THETA ** (-jnp.arange(0, ROT_DIM, 2, dtype=F32) / ROT_DIM)
    ang = jnp.arange(seq, dtype=F32)[:, None] * inv_freq[None, :]
    cos, sin = jnp.cos(ang), jnp.sin(ang)
    zeros = lambda n: jnp.zeros((seq, n), F32)
    c = jnp.concatenate([cos, cos, jnp.ones((seq, HEAD_DIM - ROT_DIM), F32)], axis=1)
    sa = jnp.concatenate([-sin, zeros(HEAD_DIM - half)], axis=1)
    sb = jnp.concatenate([zeros(half), sin, zeros(HEAD_DIM - ROT_DIM)], axis=1)
    return c, sa, sb


def kernel(x, ln_g, ln_b, sb_w_qkv, sb_w_o, kv_w, diff_w_q, diff_lambda, diff_subln_g,
           diff_w_o, mlp_w_up, mlp_w_down):
    batch, seq, d_model = x.shape
    depth = ln_g.shape[0]
    n_a = sb_w_qkv.shape[0]
    assert depth - n_a == 1
    alpha = (2 * depth) ** 0.25
    sb_heads = d_model // HEAD_DIM
    diff_heads = d_model // DIFF_V_DIM
    rope_tabs = _rope_tables(seq)

    xf = x.reshape(batch * seq, d_model)
    xb = xf
    kv_wb = kv_w.astype(BF16)
    w_up_b = mlp_w_up.astype(BF16)
    w_down_b = mlp_w_down.astype(BF16)
    for layer in range(depth):
        if layer < n_a:
            qkv = _project(xb, sb_w_qkv[layer].astype(BF16), rope_tabs, 0, d_model, seq)
            attn = _stick_breaking_attention(qkv, batch, seq, sb_heads)
            w_o = sb_w_o[layer]
        else:
            j = layer - n_a
            w_cat = jnp.concatenate([diff_w_q[j].astype(BF16), kv_wb], axis=1)
            qkv = _project(xb, w_cat, rope_tabs, 2 * d_model, d_model, seq)
            lambda_init = 0.8 - 0.6 * math.exp(-0.3 * layer)
            attn = _differential_attention(qkv, diff_lambda[j], diff_subln_g[j], batch, seq,
                                           diff_heads, lambda_init)
            w_o = diff_w_o[j]
        xf, xb = _out_proj_ln(attn, w_o.astype(BF16), xf, ln_g[layer, 0], ln_b[layer, 0], alpha)
        xf, xb = _mlp_ln(xb, w_up_b, w_down_b, layer, xf, ln_g[layer, 1], ln_b[layer, 1], alpha)
    return xf.reshape(batch, seq, d_model)
```

```python
import functools
import math

import jax
import jax.numpy as jnp
from jax import lax
from jax.experimental import pallas as pl
from jax.experimental.pallas import tpu as pltpu

HEAD_DIM = 128
DIFF_V_DIM = 256
ROT_DIM = HEAD_DIM // 4
ROPE_THETA = 500000.0
LN_EPS = 1e-5
SUBLN_EPS = 1e-5
LOGIT_SCALE_LOG2 = HEAD_DIM ** -0.5 * math.log2(math.e)
MASKED_LOGIT = -1e30
SHIFT_SLACK_LOG2 = 100.0
EXP2_ARG_MAX = 126.0

V7X_VMEM_LIMIT_BYTES = 56 * 1024 * 1024

F32 = jnp.float32
BF16 = jnp.bfloat16


def _nt_dot(a, b):
    return lax.dot_general(a, b, (((1,), (1,)), ((), ())), preferred_element_type=F32)


def _layer_norm_rows(y, g, b):
    mu = jnp.mean(y, axis=-1, keepdims=True)
    yc = y - mu
    var = jnp.mean(yc * yc, axis=-1, keepdims=True)
    return yc * lax.rsqrt(var + LN_EPS) * g + b


EPILOGUE_ROWS = 256


def _proj_kernel(x_ref, w_ref, c_ref, sa_ref, sb_ref, o_ref, *, rope_col_blocks, q_col_blocks,
                 q_scale):
    j = pl.program_id(1)
    tm, tn = o_ref.shape
    scale = jnp.where(j < q_col_blocks, q_scale, 1.0)

    def chunks(epilogue):
        for r0 in range(0, tm, EPILOGUE_ROWS):
            rows = slice(r0, r0 + EPILOGUE_ROWS)
            acc = jnp.dot(x_ref[rows, :].astype(BF16), w_ref[...], preferred_element_type=F32)
            epilogue(rows, acc * scale)

    def plain(rows, acc):
        o_ref[rows, :] = acc.astype(o_ref.dtype)

    def rotary(rows, acc):
        c = c_ref[rows, :]
        sa = sa_ref[rows, :]
        sb = sb_ref[rows, :]
        for g in range(tn // HEAD_DIM):
            t = acc[:, g * HEAD_DIM:(g + 1) * HEAD_DIM]
            rot = (t * c + pltpu.roll(t, HEAD_DIM - ROT_DIM // 2, 1) * sa
                   + pltpu.roll(t, ROT_DIM // 2, 1) * sb)
            o_ref[rows, g * HEAD_DIM:(g + 1) * HEAD_DIM] = rot.astype(o_ref.dtype)

    if rope_col_blocks == 0:
        chunks(plain)
        return

    @pl.when(j < rope_col_blocks)
    def _():
        chunks(rotary)

    @pl.when(j >= rope_col_blocks)
    def _():
        chunks(plain)


def _project(x2d, w, rope_tabs, rope_cols, q_cols, seq, *, tm=1024, tn=1024):
    m, k = x2d.shape
    n = w.shape[1]
    assert m % tm == 0 and n % tn == 0 and seq % tm == 0
    assert rope_cols % tn == 0 and q_cols % tn == 0
    row_blocks_per_seq = seq // tm
    tab_spec = pl.BlockSpec((tm, HEAD_DIM), lambda i, j: (i % row_blocks_per_seq, 0))
    return pl.pallas_call(
        functools.partial(_proj_kernel, rope_col_blocks=rope_cols // tn,
                          q_col_blocks=q_cols // tn, q_scale=LOGIT_SCALE_LOG2),
        out_shape=jax.ShapeDtypeStruct((m, n), BF16),
        grid=(m // tm, n // tn),
        in_specs=[
            pl.BlockSpec((tm, k), lambda i, j: (i, 0)),
            pl.BlockSpec((k, tn), lambda i, j: (0, j)),
            tab_spec, tab_spec, tab_spec,
        ],
        out_specs=pl.BlockSpec((tm, tn), lambda i, j: (i, j)),
        compiler_params=pltpu.CompilerParams(
            dimension_semantics=("parallel", "parallel"),
            vmem_limit_bytes=V7X_VMEM_LIMIT_BYTES),
        name="proj",
    )(x2d, w, *rope_tabs)


def _sb_logits_stage(q, k, strict):
    z = _nt_dot(q, k)
    sp = jnp.maximum(z, jnp.log2(1.0 + jnp.exp2(jnp.minimum(z, EXP2_ARG_MAX))))
    ls = z - sp
    if strict is not None:
        sp = jnp.where(strict, sp, 0.0)
    return ls, sp.astype(BF16), sp[:, :HEAD_DIM]


def _sb_weights_stage(ls, spb, sp0, v, u, acc_ref, carry_ref, r0, strict):
    groups = ls.shape[1] // HEAD_DIM
    excl = jnp.dot(spb, u, preferred_element_type=F32)
    carry = carry_ref[r0:, :]
    a = jnp.exp2(ls - jnp.concatenate([carry] * groups, axis=1) - excl)
    if strict is not None:
        a = jnp.where(strict, a, 0.0)
    acc_ref[r0:, :] += jnp.dot(a.astype(BF16), v, preferred_element_type=F32)
    carry_ref[r0:, :] = carry + (excl[:, :1] + sp0[:, :1])


def _sb_kernel(q_ref, k_ref, v_ref, u_ref, o_ref, acc_ref, carry_ref, ls_ref, spb_ref, sp0_ref,
               *, blk, chunks):
    assert chunks % 2 == 0
    qi = pl.program_id(2)
    u = u_ref[...]
    base = qi * chunks

    def k_tile(t):
        return k_ref[pl.ds(pl.multiple_of(t * blk, blk), blk), :]

    def v_tile(t):
        return v_ref[pl.ds(pl.multiple_of(t * blk, blk), blk), :]

    acc_ref[...] = jnp.zeros(acc_ref.shape, F32)
    carry_ref[...] = jnp.zeros(carry_ref.shape, F32)

    for t in reversed(range(chunks)):
        r0 = t * blk
        row = lax.broadcasted_iota(jnp.int32, (blk * chunks - r0, blk), 0)
        col = lax.broadcasted_iota(jnp.int32, (blk * chunks - r0, blk), 1)
        strict = col < row
        ls, spb, sp0 = _sb_logits_stage(q_ref[r0:, :], k_tile(base + t), strict)
        _sb_weights_stage(ls, spb, sp0, v_tile(base + t), u, acc_ref, carry_ref, r0, strict)

    def logits_to_slot(t, slot):
        ls_ref[slot], spb_ref[slot], sp0_ref[slot] = _sb_logits_stage(q_ref[...], k_tile(t), None)

    def weights_from_slot(t, slot):
        _sb_weights_stage(ls_ref[slot], spb_ref[slot], sp0_ref[slot], v_tile(t), u,
                          acc_ref, carry_ref, 0, None)

    logits_to_slot(jnp.maximum(base - 1, 0), 0)

    @pl.when(base > 0)
    def _():
        def group(first, is_last):
            for j in range(chunks):
                weights_from_slot(first - j, j % 2)
                if not (is_last and j == chunks - 1):
                    logits_to_slot(first - j - 1, (j + 1) % 2)

        odd = (qi - 1) & 1

        @pl.when(odd == 1)
        def _():
            group(base - 1, False)

        def body(i, c):
            first = base - 1 - chunks * (odd + 2 * i)
            group(first, False)
            group(first - chunks, False)
            return c

        lax.fori_loop(0, lax.shift_right_logical(qi - 1, jnp.int32(1)), body, 0)
        group(chunks - 1, True)

    o_ref[...] = acc_ref[...].astype(o_ref.dtype)


def _stick_breaking_attention(qkv, batch, seq, heads, *, blk=256, chunks=4):
    m = qkv.shape[0]
    bq = blk * chunks
    nq = seq // bq
    idx = jnp.arange(blk)
    u = (idx[:, None] > idx[None, :]).astype(BF16)
    return pl.pallas_call(
        functools.partial(_sb_kernel, blk=blk, chunks=chunks),
        out_shape=jax.ShapeDtypeStruct((m, heads * HEAD_DIM), BF16),
        grid=(batch, heads, nq),
        in_specs=[
            pl.BlockSpec((bq, HEAD_DIM), lambda b, h, i: (b * nq + i, h)),
            pl.BlockSpec((seq, HEAD_DIM), lambda b, h, i: (b, heads + h)),
            pl.BlockSpec((seq, HEAD_DIM), lambda b, h, i: (b, 2 * heads + h)),
            pl.BlockSpec((blk, blk), lambda b, h, i: (0, 0)),
        ],
        out_specs=pl.BlockSpec((bq, HEAD_DIM), lambda b, h, i: (b * nq + i, h)),
        scratch_shapes=[pltpu.VMEM((bq, HEAD_DIM), F32),
                        pltpu.VMEM((bq, HEAD_DIM), F32),
                        pltpu.VMEM((2, bq, blk), F32),
                        pltpu.VMEM((2, bq, blk), BF16),
                        pltpu.VMEM((2, bq, HEAD_DIM), F32)],
        compiler_params=pltpu.CompilerParams(
            dimension_semantics=("parallel", "parallel", "parallel"),
            vmem_limit_bytes=V7X_VMEM_LIMIT_BYTES),
        name="sb_attn",
    )(qkv, qkv, qkv, u)


def _diff_kernel(q_ref, k_ref, v_ref, lam_ref, g_ref, o_ref, m_ref, l_ref, acc_ref, kmax_ref, *,
                 blk, chunks, lambda_init):
    qi = pl.program_id(2)
    bq = blk * chunks
    wide = 2 * blk
    assert chunks == 4

    def scores(c, r0, k0, width, masked):
        lanes = slice(c * HEAD_DIM, (c + 1) * HEAD_DIM)
        s = _nt_dot(q_ref[r0:, lanes], k_ref[pl.ds(k0, width), lanes])
        if masked:
            row = lax.broadcasted_iota(jnp.int32, s.shape, 0)
            col = lax.broadcasted_iota(jnp.int32, s.shape, 1)
            s = jnp.where(col <= row, s, MASKED_LOGIT)
        return [s[:, g * HEAD_DIM:(g + 1) * HEAD_DIM] for g in range(width // HEAD_DIM)]

    def sweep(tile_fn):
        for t in range(chunks):
            tile_fn(t * blk, pl.multiple_of((qi * chunks + t) * blk, blk), blk, True)

        odd = qi & 1

        @pl.when(odd == 1)
        def _():
            for j in range(2):
                tile_fn(0, j * wide, wide, False)

        def body(i, carry):
            for j in range(4):
                tile_fn(0, pl.multiple_of((2 * odd + 4 * i + j) * wide, wide), wide, False)
            return carry

        lax.fori_loop(0, lax.shift_right_logical(qi, jnp.int32(1)), body, 0)

    def max_tile(r0, k0, width, masked):
        for c in range(2):
            m_ref[c, r0:, :] = functools.reduce(
                jnp.maximum, scores(c, r0, k0, width, masked), m_ref[c, r0:, :])

    def pv_tile(r0, k0, width, masked):
        v = v_ref[pl.ds(k0, width), :]
        for c in range(2):
            m = m_ref[c, r0:, :]
            p = [jnp.exp2(s - m) for s in scores(c, r0, k0, width, masked)]
            l_ref[c, r0:, :] += functools.reduce(jnp.add, p)
            acc_ref[c, r0:, :] += jnp.dot(jnp.concatenate(p, axis=1).astype(BF16), v,
                                          preferred_element_type=F32)

    half_r = lax.broadcasted_iota(jnp.int32, (2 * HEAD_DIM, 2 * HEAD_DIM), 0) >= HEAD_DIM
    half_c = lax.broadcasted_iota(jnp.int32, (2 * HEAD_DIM, 2 * HEAD_DIM), 1) >= HEAD_DIM
    ones_bd = jnp.where(half_r == half_c, 1.0, 0.0).astype(BF16)

    def sq_norms(x):
        return jnp.dot(x * x, ones_bd, preferred_element_type=F32)

    @pl.when(qi == 0)
    def _():
        def body(i, best):
            return jnp.maximum(
                best, sq_norms(k_ref[pl.ds(pl.multiple_of(i * wide, wide), wide), :]))

        best = lax.fori_loop(0, k_ref.shape[0] // wide, body,
                             jnp.zeros((wide, 2 * HEAD_DIM), F32))
        kmax_ref[...] = jnp.broadcast_to(jnp.max(best, axis=0, keepdims=True), kmax_ref.shape)

    bound = jnp.sqrt(sq_norms(q_ref[...]) * kmax_ref[:1, :])
    for c in range(2):
        m_ref[c] = bound[:, c * HEAD_DIM:(c + 1) * HEAD_DIM]
    bound_ok = 2.0 * jnp.max(bound) <= SHIFT_SLACK_LOG2

    @pl.when(jnp.logical_not(bound_ok))
    def _():
        m_ref[...] = jnp.full(m_ref.shape, MASKED_LOGIT, F32)
        sweep(max_tile)
        for c in range(2):
            m_ref[c] = jnp.broadcast_to(jnp.max(m_ref[c], axis=1, keepdims=True),
                                        m_ref.shape[1:])

    l_ref[...] = jnp.zeros(l_ref.shape, F32)
    acc_ref[...] = jnp.zeros(acc_ref.shape, F32)
    sweep(pv_tile)

    lam = lam_ref[...]
    lam_full = (jnp.exp(jnp.sum(lam[0:1] * lam[1:2], axis=1, keepdims=True))
                - jnp.exp(jnp.sum(lam[2:3] * lam[3:4], axis=1, keepdims=True)) + lambda_init)
    inv_l = [1.0 / jnp.sum(l_ref[c], axis=1, keepdims=True) for c in range(2)]
    o = acc_ref[0] * inv_l[0] - lam_full * (acc_ref[1] * inv_l[1])
    o = o * lax.rsqrt(jnp.mean(o * o, axis=-1, keepdims=True) + SUBLN_EPS)
    o_ref[...] = (o * (g_ref[...] * (1.0 - lambda_init))).astype(o_ref.dtype)


def _differential_attention(qkv, lam, subln_g, batch, seq, heads, lambda_init, *, blk=256,
                            chunks=4):
    m = qkv.shape[0]
    bq = blk * chunks
    nq = seq // bq
    w = DIFF_V_DIM
    return pl.pallas_call(
        functools.partial(_diff_kernel, blk=blk, chunks=chunks, lambda_init=lambda_init),
        out_shape=jax.ShapeDtypeStruct((m, heads * w), BF16),
        grid=(batch, heads, nq),
        in_specs=[
            pl.BlockSpec((bq, w), lambda b, h, i: (b * nq + i, h)),
            pl.BlockSpec((seq, w), lambda b, h, i: (b, heads + h)),
            pl.BlockSpec((seq, w), lambda b, h, i: (b, 2 * heads + h)),
            pl.BlockSpec((4, HEAD_DIM), lambda b, h, i: (0, 0)),
            pl.BlockSpec((1, w), lambda b, h, i: (0, 0)),
        ],
        out_specs=pl.BlockSpec((bq, w), lambda b, h, i: (b * nq + i, h)),
        scratch_shapes=[pltpu.VMEM((2, bq, HEAD_DIM), F32), pltpu.VMEM((2, bq, HEAD_DIM), F32),
                        pltpu.VMEM((2, bq, w), F32), pltpu.VMEM((8, 2 * HEAD_DIM), F32)],
        compiler_params=pltpu.CompilerParams(
            dimension_semantics=("parallel", "parallel", "arbitrary"),
            vmem_limit_bytes=V7X_VMEM_LIMIT_BYTES),
        name="diff_attn",
    )(qkv, qkv, qkv, lam, subln_g.reshape(1, w))


def _out_ln_kernel(a_ref, w_ref, x_ref, g_ref, b_ref, of_ref, ob_ref, *, alpha):
    for r0 in range(0, of_ref.shape[0], EPILOGUE_ROWS):
        rows = slice(r0, r0 + EPILOGUE_ROWS)
        h = jnp.dot(a_ref[rows, :], w_ref[...], preferred_element_type=F32)
        y = _layer_norm_rows(alpha * x_ref[rows, :] + h, g_ref[...], b_ref[...])
        of_ref[rows, :] = y
        ob_ref[rows, :] = y.astype(BF16)


def _out_proj_ln(attn, w, x, g, b, alpha, *, tm=512):
    m, k = attn.shape
    d = w.shape[1]
    row = pl.BlockSpec((tm, d), lambda i: (i, 0))
    vec = pl.BlockSpec((1, d), lambda i: (0, 0))
    return pl.pallas_call(
        functools.partial(_out_ln_kernel, alpha=alpha),
        out_shape=(jax.ShapeDtypeStruct((m, d), F32), jax.ShapeDtypeStruct((m, d), BF16)),
        grid=(m // tm,),
        in_specs=[pl.BlockSpec((tm, k), lambda i: (i, 0)),
                  pl.BlockSpec((k, d), lambda i: (0, 0)),
                  row, vec, vec],
        out_specs=(row, row),
        compiler_params=pltpu.CompilerParams(
            dimension_semantics=("parallel",),
            vmem_limit_bytes=V7X_VMEM_LIMIT_BYTES),
        name="out_proj_ln",
    )(attn, w, x, g.reshape(1, d), b.reshape(1, d))


def _mlp_ln_kernel(xb_ref, wu_ref, wd_ref, x_ref, g_ref, b_ref, of_ref, ob_ref, acc_ref, *, alpha):
    j = pl.program_id(1)

    @pl.when(j == 0)
    def _():
        acc_ref[...] = jnp.zeros_like(acc_ref)

    h = jnp.maximum(jnp.dot(xb_ref[...], wu_ref[...], preferred_element_type=F32), 0.0)
    acc_ref[...] += jnp.dot((h * h).astype(BF16), wd_ref[...], preferred_element_type=F32)

    @pl.when(j == pl.num_programs(1) - 1)
    def _():
        y = _layer_norm_rows(alpha * x_ref[...] + acc_ref[...], g_ref[...], b_ref[...])
        of_ref[...] = y
        ob_ref[...] = y.astype(BF16)


MLP_FF_TILE = 1024


def _mlp_ln(xb, w_up_all, w_down_all, layer, x, g, b, alpha, *, tm=512, tf=MLP_FF_TILE):
    m, d = xb.shape
    f = w_up_all.shape[2]
    row = lambda: pl.BlockSpec((tm, d), lambda i, j: (i, 0))
    vec = pl.BlockSpec((1, d), lambda i, j: (0, 0))
    return pl.pallas_call(
        functools.partial(_mlp_ln_kernel, alpha=alpha),
        out_shape=(jax.ShapeDtypeStruct((m, d), F32), jax.ShapeDtypeStruct((m, d), BF16)),
        grid=(m // tm, f // tf),
        in_specs=[row(),
                  pl.BlockSpec((None, d, tf), lambda i, j: (layer, 0, j)),
                  pl.BlockSpec((None, tf, d), lambda i, j: (layer, j, 0)),
                  row(), vec, vec],
        out_specs=(row(), row()),
        scratch_shapes=[pltpu.VMEM((tm, d), F32)],
        compiler_params=pltpu.CompilerParams(
            dimension_semantics=("parallel", "arbitrary"),
            vmem_limit_bytes=V7X_VMEM_LIMIT_BYTES),
        name="mlp_ln",
    )(xb, w_up_all, w_down_all, x, g.reshape(1, d), b.reshape(1, d))


def _rope_tables(seq):
    half = ROT_DIM // 2
    inv_freq = ROPE_THETA ** (-jnp.arange(0, ROT_DIM, 2, dtype=F32) / ROT_DIM)
    ang = jnp.arange(seq, dtype=F32)[:, None] * inv_freq[None, :]
    cos, sin = jnp.cos(ang), jnp.sin(ang)
    zeros = lambda n: jnp.zeros((seq, n), F32)
    c = jnp.concatenate([cos, cos, jnp.ones((seq, HEAD_DIM - ROT_DIM), F32)], axis=1)
    sa = jnp.concatenate([-sin, zeros(HEAD_DIM - half)], axis=1)
    sb = jnp.concatenate([zeros(half), sin, zeros(HEAD_DIM - ROT_DIM)], axis=1)
    return c, sa, sb


def kernel(x, ln_g, ln_b, sb_w_qkv, sb_w_o, kv_w, diff_w_q, diff_lambda, diff_subln_g,
           diff_w_o, mlp_w_up, mlp_w_down):
    batch, seq, d_model = x.shape
    depth = ln_g.shape[0]
    n_a = sb_w_qkv.shape[0]
    assert depth - n_a == 1
    alpha = (2 * depth) ** 0.25
    sb_heads = d_model // HEAD_DIM
    diff_heads = d_model // DIFF_V_DIM
    rope_tabs = _rope_tables(seq)

    xf = x.reshape(batch * seq, d_model)
    xb = xf
    kv_wb = kv_w.astype(BF16)
    w_up_b = mlp_w_up.astype(BF16)
    w_down_b = mlp_w_down.astype(BF16)
    for layer in range(depth):
        if layer < n_a:
            qkv = _project(xb, sb_w_qkv[layer].astype(BF16), rope_tabs, 0, d_model, seq)
            attn = _stick_breaking_attention(qkv, batch, seq, sb_heads)
            w_o = sb_w_o[layer]
        else:
            j = layer - n_a
            w_cat = jnp.concatenate([diff_w_q[j].astype(BF16), kv_wb], axis=1)
            qkv = _project(xb, w_cat, rope_tabs, 2 * d_model, d_model, seq)
            lambda_init = 0.8 - 0.6 * math.exp(-0.3 * layer)
            attn = _differential_attention(qkv, diff_lambda[j], diff_subln_g[j], batch, seq,
                                           diff_heads, lambda_init)
            w_o = diff_w_o[j]
        xf, xb = _out_proj_ln(attn, w_o.astype(BF16), xf, ln_g[layer, 0], ln_b[layer, 0], alpha)
        xf, xb = _mlp_ln(xb, w_up_b, w_down_b, layer, xf, ln_g[layer, 1], ln_b[layer, 1], alpha)
    return xf.reshape(batch, seq, d_model)
```

```python
import functools
import math

import jax
import jax.numpy as jnp
from jax import lax
from jax.experimental import pallas as pl
from jax.experimental.pallas import tpu as pltpu

HEAD_DIM = 128
DIFF_V_DIM = 256
ROT_DIM = HEAD_DIM // 4
ROPE_THETA = 500000.0
LN_EPS = 1e-5
SUBLN_EPS = 1e-5
LOGIT_SCALE_LOG2 = HEAD_DIM ** -0.5 * math.log2(math.e)
MASKED_LOGIT = -1e30
SHIFT_SLACK_LOG2 = 100.0
EXP2_ARG_MAX = 126.0
SB_DEAD_CARRY_LOG2 = 150.0

V7X_VMEM_LIMIT_BYTES = 56 * 1024 * 1024

F32 = jnp.float32
BF16 = jnp.bfloat16


def _nt_dot(a, b):
    return lax.dot_general(a, b, (((1,), (1,)), ((), ())), preferred_element_type=F32)


def _layer_norm_rows(y, g, b):
    mu = jnp.mean(y, axis=-1, keepdims=True)
    yc = y - mu
    var = jnp.mean(yc * yc, axis=-1, keepdims=True)
    return yc * lax.rsqrt(var + LN_EPS) * g + b


EPILOGUE_ROWS = 256


def _proj_kernel(x_ref, w_ref, c_ref, sa_ref, sb_ref, o_ref, *, rope_col_blocks, q_col_blocks,
                 q_scale):
    j = pl.program_id(1)
    tm, tn = o_ref.shape
    scale = jnp.where(j < q_col_blocks, q_scale, 1.0)

    def chunks(epilogue):
        for r0 in range(0, tm, EPILOGUE_ROWS):
            rows = slice(r0, r0 + EPILOGUE_ROWS)
            acc = jnp.dot(x_ref[rows, :].astype(BF16), w_ref[...], preferred_element_type=F32)
            epilogue(rows, acc * scale)

    def plain(rows, acc):
        o_ref[rows, :] = acc.astype(o_ref.dtype)

    def rotary(rows, acc):
        c = c_ref[rows, :]
        sa = sa_ref[rows, :]
        sb = sb_ref[rows, :]
        for g in range(tn // HEAD_DIM):
            t = acc[:, g * HEAD_DIM:(g + 1) * HEAD_DIM]
            rot = (t * c + pltpu.roll(t, HEAD_DIM - ROT_DIM // 2, 1) * sa
                   + pltpu.roll(t, ROT_DIM // 2, 1) * sb)
            o_ref[rows, g * HEAD_DIM:(g + 1) * HEAD_DIM] = rot.astype(o_ref.dtype)

    if rope_col_blocks == 0:
        chunks(plain)
        return

    @pl.when(j < rope_col_blocks)
    def _():
        chunks(rotary)

    @pl.when(j >= rope_col_blocks)
    def _():
        chunks(plain)


def _project(x2d, w, rope_tabs, rope_cols, q_cols, seq, *, tm=1024, tn=1024):
    m, k = x2d.shape
    n = w.shape[1]
    assert m % tm == 0 and n % tn == 0 and seq % tm == 0
    assert rope_cols % tn == 0 and q_cols % tn == 0
    row_blocks_per_seq = seq // tm
    tab_spec = pl.BlockSpec((tm, HEAD_DIM), lambda i, j: (i % row_blocks_per_seq, 0))
    return pl.pallas_call(
        functools.partial(_proj_kernel, rope_col_blocks=rope_cols // tn,
                          q_col_blocks=q_cols // tn, q_scale=LOGIT_SCALE_LOG2),
        out_shape=jax.ShapeDtypeStruct((m, n), BF16),
        grid=(m // tm, n // tn),
        in_specs=[
            pl.BlockSpec((tm, k), lambda i, j: (i, 0)),
            pl.BlockSpec((k, tn), lambda i, j: (0, j)),
            tab_spec, tab_spec, tab_spec,
        ],
        out_specs=pl.BlockSpec((tm, tn), lambda i, j: (i, j)),
        compiler_params=pltpu.CompilerParams(
            dimension_semantics=("parallel", "parallel"),
            vmem_limit_bytes=V7X_VMEM_LIMIT_BYTES),
        name="proj",
    )(x2d, w, *rope_tabs)


def _sb_logits_stage(q, k, strict):
    z = _nt_dot(q, k)
    sp = jnp.maximum(z, jnp.log2(1.0 + jnp.exp2(jnp.minimum(z, EXP2_ARG_MAX))))
    ls = z - sp
    if strict is not None:
        sp = jnp.where(strict, sp, 0.0)
    return ls, sp.astype(BF16), sp[:, :HEAD_DIM]


def _sb_weights_stage(ls, spb, sp0, v, u, acc_ref, carry_ref, r0, strict):
    groups = ls.shape[1] // HEAD_DIM
    excl = jnp.dot(spb, u, preferred_element_type=F32)
    carry = carry_ref[r0:, :]
    a = jnp.exp2(ls - jnp.concatenate([carry] * groups, axis=1) - excl)
    if strict is not None:
        a = jnp.where(strict, a, 0.0)
    acc_ref[r0:, :] += jnp.dot(a.astype(BF16), v, preferred_element_type=F32)
    carry_ref[r0:, :] = carry + (excl[:, :1] + sp0[:, :1])


def _sb_kernel(q_ref, k_ref, v_ref, u_ref, o_ref, acc_ref, carry_ref, ls_ref, spb_ref, sp0_ref,
               *, blk, chunks):
    assert chunks % 2 == 0
    qi = pl.program_id(2)
    u = u_ref[...]
    base = qi * chunks

    def k_tile(t):
        return k_ref[pl.ds(pl.multiple_of(t * blk, blk), blk), :]

    def v_tile(t):
        return v_ref[pl.ds(pl.multiple_of(t * blk, blk), blk), :]

    acc_ref[...] = jnp.zeros(acc_ref.shape, F32)
    carry_ref[...] = jnp.zeros(carry_ref.shape, F32)

    for t in reversed(range(chunks)):
        r0 = t * blk
        row = lax.broadcasted_iota(jnp.int32, (blk * chunks - r0, blk), 0)
        col = lax.broadcasted_iota(jnp.int32, (blk * chunks - r0, blk), 1)
        strict = col < row
        ls, spb, sp0 = _sb_logits_stage(q_ref[r0:, :], k_tile(base + t), strict)
        _sb_weights_stage(ls, spb, sp0, v_tile(base + t), u, acc_ref, carry_ref, r0, strict)

    def logits_to_slot(t, slot):
        ls_ref[slot], spb_ref[slot], sp0_ref[slot] = _sb_logits_stage(q_ref[...], k_tile(t), None)

    def weights_from_slot(t, slot):
        _sb_weights_stage(ls_ref[slot], spb_ref[slot], sp0_ref[slot], v_tile(t), u,
                          acc_ref, carry_ref, 0, None)

    def position(p, par):
        weights_from_slot(base - 1 - p, par)
        logits_to_slot(jnp.maximum(base - 2 - p, 0), 1 - par)

    def max_weight_exponent():
        return -jnp.min(carry_ref[...])

    logits_to_slot(jnp.maximum(base - 1, 0), 0)

    @pl.when(base > 0)
    def _():
        position(0, 0)
        after_first = max_weight_exponent()

        def rest_of_group():
            for p in range(1, chunks):
                position(p, p % 2)
            return max_weight_exponent()

        after_group = lax.cond(after_first > -SB_DEAD_CARRY_LOG2, rest_of_group,
                               lambda: after_first)

        def cond(state):
            g, exponent = state
            return jnp.logical_and(g < qi, exponent > -SB_DEAD_CARRY_LOG2)

        def body(state):
            g, _ = state
            for j in range(chunks):
                position(g * chunks + j, j % 2)
            return g + 1, max_weight_exponent()

        lax.while_loop(cond, body, (jnp.int32(1), after_group))

    o_ref[...] = acc_ref[...].astype(o_ref.dtype)


def _stick_breaking_attention(qkv, batch, seq, heads, *, blk=256, chunks=4):
    m = qkv.shape[0]
    bq = blk * chunks
    nq = seq // bq
    idx = jnp.arange(blk)
    u = (idx[:, None] > idx[None, :]).astype(BF16)
    return pl.pallas_call(
        functools.partial(_sb_kernel, blk=blk, chunks=chunks),
        out_shape=jax.ShapeDtypeStruct((m, heads * HEAD_DIM), BF16),
        grid=(batch, heads, nq),
        in_specs=[
            pl.BlockSpec((bq, HEAD_DIM), lambda b, h, i: (b * nq + i, h)),
            pl.BlockSpec((seq, HEAD_DIM), lambda b, h, i: (b, heads + h)),
            pl.BlockSpec((seq, HEAD_DIM), lambda b, h, i: (b, 2 * heads + h)),
            pl.BlockSpec((blk, blk), lambda b, h, i: (0, 0)),
        ],
        out_specs=pl.BlockSpec((bq, HEAD_DIM), lambda b, h, i: (b * nq + i, h)),
        scratch_shapes=[pltpu.VMEM((bq, HEAD_DIM), F32),
                        pltpu.VMEM((bq, HEAD_DIM), F32),
                        pltpu.VMEM((2, bq, blk), F32),
                        pltpu.VMEM((2, bq, blk), BF16),
                        pltpu.VMEM((2, bq, HEAD_DIM), F32)],
        compiler_params=pltpu.CompilerParams(
            dimension_semantics=("parallel", "parallel", "parallel"),
            vmem_limit_bytes=V7X_VMEM_LIMIT_BYTES),
        name="sb_attn",
    )(qkv, qkv, qkv, u)


def _diff_kernel(q_ref, k_ref, v_ref, lam_ref, g_ref, o_ref, m_ref, l_ref, acc_ref, kmax_ref, *,
                 blk, chunks, lambda_init):
    qi = pl.program_id(2)
    bq = blk * chunks
    wide = 2 * blk
    assert chunks == 4

    def scores(c, r0, k0, width, masked):
        lanes = slice(c * HEAD_DIM, (c + 1) * HEAD_DIM)
        s = _nt_dot(q_ref[r0:, lanes], k_ref[pl.ds(k0, width), lanes])
        if masked:
            row = lax.broadcasted_iota(jnp.int32, s.shape, 0)
            col = lax.broadcasted_iota(jnp.int32, s.shape, 1)
            s = jnp.where(col <= row, s, MASKED_LOGIT)
        return [s[:, g * HEAD_DIM:(g + 1) * HEAD_DIM] for g in range(width // HEAD_DIM)]

    def sweep(tile_fn):
        odd = qi & 1

        @pl.when(odd == 1)
        def _():
            for j in range(2):
                tile_fn(0, j * wide, wide, False)

        def body(i, carry):
            for j in range(4):
                tile_fn(0, pl.multiple_of((2 * odd + 4 * i + j) * wide, wide), wide, False)
            return carry

        lax.fori_loop(0, lax.shift_right_logical(qi, jnp.int32(1)), body, 0)
        for t in range(chunks):
            tile_fn(t * blk, pl.multiple_of((qi * chunks + t) * blk, blk), blk, True)

    def max_tile(r0, k0, width, masked):
        for c in range(2):
            m_ref[c, r0:, :] = functools.reduce(
                jnp.maximum, scores(c, r0, k0, width, masked), m_ref[c, r0:, :])

    def pv_tile(r0, k0, width, masked):
        v = v_ref[pl.ds(k0, width), :]
        for c in range(2):
            m = m_ref[c, r0:, :]
            p = [jnp.exp2(s - m) for s in scores(c, r0, k0, width, masked)]
            l_ref[c, r0:, :] += functools.reduce(jnp.add, p)
            acc_ref[c, r0:, :] += jnp.dot(jnp.concatenate(p, axis=1).astype(BF16), v,
                                          preferred_element_type=F32)

    half_r = lax.broadcasted_iota(jnp.int32, (2 * HEAD_DIM, 2 * HEAD_DIM), 0) >= HEAD_DIM
    half_c = lax.broadcasted_iota(jnp.int32, (2 * HEAD_DIM, 2 * HEAD_DIM), 1) >= HEAD_DIM
    ones_bd = jnp.where(half_r == half_c, 1.0, 0.0).astype(BF16)

    def sq_norms(x):
        return jnp.dot(x * x, ones_bd, preferred_element_type=F32)

    @pl.when(qi == 0)
    def _():
        def body(i, best):
            return jnp.maximum(
                best, sq_norms(k_ref[pl.ds(pl.multiple_of(i * wide, wide), wide), :]))

        best = lax.fori_loop(0, k_ref.shape[0] // wide, body,
                             jnp.zeros((wide, 2 * HEAD_DIM), F32))
        kmax_ref[...] = jnp.broadcast_to(jnp.max(best, axis=0, keepdims=True), kmax_ref.shape)

    bound = jnp.sqrt(sq_norms(q_ref[...]) * kmax_ref[:1, :])
    for c in range(2):
        m_ref[c] = bound[:, c * HEAD_DIM:(c + 1) * HEAD_DIM]
    bound_ok = 2.0 * jnp.max(bound) <= SHIFT_SLACK_LOG2

    @pl.when(jnp.logical_not(bound_ok))
    def _():
        m_ref[...] = jnp.full(m_ref.shape, MASKED_LOGIT, F32)
        sweep(max_tile)
        for c in range(2):
            m_ref[c] = jnp.broadcast_to(jnp.max(m_ref[c], axis=1, keepdims=True),
                                        m_ref.shape[1:])

    l_ref[...] = jnp.zeros(l_ref.shape, F32)
    acc_ref[...] = jnp.zeros(acc_ref.shape, F32)
    sweep(pv_tile)

    lam = lam_ref[...]
    lam_full = (jnp.exp(jnp.sum(lam[0:1] * lam[1:2], axis=1, keepdims=True))
                - jnp.exp(jnp.sum(lam[2:3] * lam[3:4], axis=1, keepdims=True)) + lambda_init)
    inv_l = [1.0 / jnp.sum(l_ref[c], axis=1, keepdims=True) for c in range(2)]
    o = acc_ref[0] * inv_l[0] - lam_full * (acc_ref[1] * inv_l[1])
    o = o * lax.rsqrt(jnp.mean(o * o, axis=-1, keepdims=True) + SUBLN_EPS)
    o_ref[...] = (o * (g_ref[...] * (1.0 - lambda_init))).astype(o_ref.dtype)


def _differential_attention(qkv, lam, subln_g, batch, seq, heads, lambda_init, *, blk=256,
                            chunks=4):
    m = qkv.shape[0]
    bq = blk * chunks
    nq = seq // bq
    w = DIFF_V_DIM
    return pl.pallas_call(
        functools.partial(_diff_kernel, blk=blk, chunks=chunks, lambda_init=lambda_init),
        out_shape=jax.ShapeDtypeStruct((m, heads * w), BF16),
        grid=(batch, heads, nq),
        in_specs=[
            pl.BlockSpec((bq, w), lambda b, h, i: (b * nq + i, h)),
            pl.BlockSpec((seq, w), lambda b, h, i: (b, heads + h)),
            pl.BlockSpec((seq, w), lambda b, h, i: (b, 2 * heads + h)),
            pl.BlockSpec((4, HEAD_DIM), lambda b, h, i: (0, 0)),
            pl.BlockSpec((1, w), lambda b, h, i: (0, 0)),
        ],
        out_specs=pl.BlockSpec((bq, w), lambda b, h, i: (b * nq + i, h)),
        scratch_shapes=[pltpu.VMEM((2, bq, HEAD_DIM), F32), pltpu.VMEM((2, bq, HEAD_DIM), F32),
                        pltpu.VMEM((2, bq, w), F32), pltpu.VMEM((8, 2 * HEAD_DIM), F32)],
        compiler_params=pltpu.CompilerParams(
            dimension_semantics=("parallel", "parallel", "arbitrary"),
            vmem_limit_bytes=V7X_VMEM_LIMIT_BYTES),
        name="diff_attn",
    )(qkv, qkv, qkv, lam, subln_g.reshape(1, w))


def _out_ln_kernel(a_ref, w_ref, x_ref, g_ref, b_ref, of_ref, ob_ref, *, alpha):
    for r0 in range(0, of_ref.shape[0], EPILOGUE_ROWS):
        rows = slice(r0, r0 + EPILOGUE_ROWS)
        h = jnp.dot(a_ref[rows, :], w_ref[...], preferred_element_type=F32)
        y = _layer_norm_rows(alpha * x_ref[rows, :] + h, g_ref[...], b_ref[...])
        of_ref[rows, :] = y
        ob_ref[rows, :] = y.astype(BF16)


def _out_proj_ln(attn, w, x, g, b, alpha, *, tm=512):
    m, k = attn.shape
    d = w.shape[1]
    row = pl.BlockSpec((tm, d), lambda i: (i, 0))
    vec = pl.BlockSpec((1, d), lambda i: (0, 0))
    return pl.pallas_call(
        functools.partial(_out_ln_kernel, alpha=alpha),
        out_shape=(jax.ShapeDtypeStruct((m, d), F32), jax.ShapeDtypeStruct((m, d), BF16)),
        grid=(m // tm,),
        in_specs=[pl.BlockSpec((tm, k), lambda i: (i, 0)),
                  pl.BlockSpec((k, d), lambda i: (0, 0)),
                  row, vec, vec],
        out_specs=(row, row),
        compiler_params=pltpu.CompilerParams(
            dimension_semantics=("parallel",),
            vmem_limit_bytes=V7X_VMEM_LIMIT_BYTES),
        name="out_proj_ln",
    )(attn, w, x, g.reshape(1, d), b.reshape(1, d))


def _mlp_ln_kernel(xb_ref, wu_ref, wd_ref, x_ref, g_ref, b_ref, of_ref, ob_ref, acc_ref, *, alpha):
    j = pl.program_id(1)

    @pl.when(j == 0)
    def _():
        acc_ref[...] = jnp.zeros_like(acc_ref)

    h = jnp.maximum(jnp.dot(xb_ref[...], wu_ref[...], preferred_element_type=F32), 0.0)
    acc_ref[...] += jnp.dot((h * h).astype(BF16), wd_ref[...], preferred_element_type=F32)

    @pl.when(j == pl.num_programs(1) - 1)
    def _():
        y = _layer_norm_rows(alpha * x_ref[...] + acc_ref[...], g_ref[...], b_ref[...])
        of_ref[...] = y
        ob_ref[...] = y.astype(BF16)


MLP_FF_TILE = 1024


def _mlp_ln(xb, w_up_all, w_down_all, layer, x, g, b, alpha, *, tm=512, tf=MLP_FF_TILE):
    m, d = xb.shape
    f = w_up_all.shape[2]
    row = lambda: pl.BlockSpec((tm, d), lambda i, j: (i, 0))
    vec = pl.BlockSpec((1, d), lambda i, j: (0, 0))
    return pl.pallas_call(
        functools.partial(_mlp_ln_kernel, alpha=alpha),
        out_shape=(jax.ShapeDtypeStruct((m, d), F32), jax.ShapeDtypeStruct((m, d), BF16)),
        grid=(m // tm, f // tf),
        in_specs=[row(),
                  pl.BlockSpec((None, d, tf), lambda i, j: (layer, 0, j)),
                  pl.BlockSpec((None, tf, d), lambda i, j: (layer, j, 0)),
                  row(), vec, vec],
        out_specs=(row(), row()),
        scratch_shapes=[pltpu.VMEM((tm, d), F32)],
        compiler_params=pltpu.CompilerParams(
            dimension_semantics=("parallel", "arbitrary"),
            vmem_limit_bytes=V7X_VMEM_LIMIT_BYTES),
        name="mlp_ln",
    )(xb, w_up_all, w_down_all, x, g.reshape(1, d), b.reshape(1, d))


def _rope_tables(seq):
    half = ROT_DIM // 2
    inv_freq = ROPE_THETA ** (-jnp.arange(0, ROT_DIM, 2, dtype=F32) / ROT_DIM)
    ang = jnp.arange(seq, dtype=F32)[:, None] * inv_freq[None, :]
    cos, sin = jnp.cos(ang), jnp.sin(ang)
    zeros = lambda n: jnp.zeros((seq, n), F32)
    c = jnp.concatenate([cos, cos, jnp.ones((seq, HEAD_DIM - ROT_DIM), F32)], axis=1)
    sa = jnp.concatenate([-sin, zeros(HEAD_DIM - half)], axis=1)
    sb = jnp.concatenate([zeros(half), sin, zeros(HEAD_DIM - ROT_DIM)], axis=1)
    return c, sa, sb


def kernel(x, ln_g, ln_b, sb_w_qkv, sb_w_o, kv_w, diff_w_q, diff_lambda, diff_subln_g,
           diff_w_o, mlp_w_up, mlp_w_down):
    batch, seq, d_model = x.shape
    depth = ln_g.shape[0]
    n_a = sb_w_qkv.shape[0]
    assert depth - n_a == 1
    alpha = (2 * depth) ** 0.25
    sb_heads = d_model // HEAD_DIM
    diff_heads = d_model // DIFF_V_DIM
    rope_tabs = _rope_tables(seq)

    xf = x.reshape(batch * seq, d_model)
    xb = xf
    kv_wb = kv_w.astype(BF16)
    w_up_b = mlp_w_up.astype(BF16)
    w_down_b = mlp_w_down.astype(BF16)
    for layer in range(depth):
        if layer < n_a:
            qkv = _project(xb, sb_w_qkv[layer].astype(BF16), rope_tabs, 0, d_model, seq)
            attn = _stick_breaking_attention(qkv, batch, seq, sb_heads)
            w_o = sb_w_o[layer]
        else:
            j = layer - n_a
            w_cat = jnp.concatenate([diff_w_q[j].astype(BF16), kv_wb], axis=1)
            qkv = _project(xb, w_cat, rope_tabs, 2 * d_model, d_model, seq)
            lambda_init = 0.8 - 0.6 * math.exp(-0.3 * layer)
            attn = _differential_attention(qkv, diff_lambda[j], diff_subln_g[j], batch, seq,
                                           diff_heads, lambda_init)
            w_o = diff_w_o[j]
        xf, xb = _out_proj_ln(attn, w_o.astype(BF16), xf, ln_g[layer, 0], ln_b[layer, 0], alpha)
        xf, xb = _mlp_ln(xb, w_up_b, w_down_b, layer, xf, ln_g[layer, 1], ln_b[layer, 1], alpha)
    return xf.reshape(batch, seq, d_model)
```

```python
import functools
import math

import jax
import jax.numpy as jnp
from jax import lax
from jax.experimental import pallas as pl
from jax.experimental.pallas import tpu as pltpu

HEAD_DIM = 128
DIFF_V_DIM = 256
ROT_DIM = HEAD_DIM // 4
ROPE_THETA = 500000.0
LN_EPS = 1e-5
SUBLN_EPS = 1e-5
LOGIT_SCALE_LOG2 = HEAD_DIM ** -0.5 * math.log2(math.e)
MASKED_LOGIT = -1e30
SHIFT_SLACK_LOG2 = 100.0
EXP2_ARG_MAX = 126.0
SB_DEAD_CARRY_LOG2 = 150.0

V7X_VMEM_LIMIT_BYTES = 56 * 1024 * 1024

F32 = jnp.float32
BF16 = jnp.bfloat16


def _nt_dot(a, b):
    return lax.dot_general(a, b, (((1,), (1,)), ((), ())), preferred_element_type=F32)


def _layer_norm_rows(y, g, b):
    mu = jnp.mean(y, axis=-1, keepdims=True)
    yc = y - mu
    var = jnp.mean(yc * yc, axis=-1, keepdims=True)
    return yc * lax.rsqrt(var + LN_EPS) * g + b


EPILOGUE_ROWS = 256


def _proj_kernel(x_ref, w_ref, c_ref, sa_ref, sb_ref, o_ref, *, rope_col_blocks, q_col_blocks,
                 q_scale):
    j = pl.program_id(1)
    tm, tn = o_ref.shape
    scale = jnp.where(j < q_col_blocks, q_scale, 1.0)

    def chunks(epilogue):
        for r0 in range(0, tm, EPILOGUE_ROWS):
            rows = slice(r0, r0 + EPILOGUE_ROWS)
            acc = jnp.dot(x_ref[rows, :].astype(BF16), w_ref[...], preferred_element_type=F32)
            epilogue(rows, acc * scale)

    def plain(rows, acc):
        o_ref[rows, :] = acc.astype(o_ref.dtype)

    def rotary(rows, acc):
        c = c_ref[rows, :]
        sa = sa_ref[rows, :]
        sb = sb_ref[rows, :]
        for g in range(tn // HEAD_DIM):
            t = acc[:, g * HEAD_DIM:(g + 1) * HEAD_DIM]
            rot = (t * c + pltpu.roll(t, HEAD_DIM - ROT_DIM // 2, 1) * sa
                   + pltpu.roll(t, ROT_DIM // 2, 1) * sb)
            o_ref[rows, g * HEAD_DIM:(g + 1) * HEAD_DIM] = rot.astype(o_ref.dtype)

    if rope_col_blocks == 0:
        chunks(plain)
        return

    @pl.when(j < rope_col_blocks)
    def _():
        chunks(rotary)

    @pl.when(j >= rope_col_blocks)
    def _():
        chunks(plain)


def _project(x2d, w, rope_tabs, rope_cols, q_cols, seq, *, tm=1024, tn=1024):
    m, k = x2d.shape
    n = w.shape[1]
    assert m % tm == 0 and n % tn == 0 and seq % tm == 0
    assert rope_cols % tn == 0 and q_cols % tn == 0
    row_blocks_per_seq = seq // tm
    tab_spec = pl.BlockSpec((tm, HEAD_DIM), lambda i, j: (i % row_blocks_per_seq, 0))
    return pl.pallas_call(
        functools.partial(_proj_kernel, rope_col_blocks=rope_cols // tn,
                          q_col_blocks=q_cols // tn, q_scale=LOGIT_SCALE_LOG2),
        out_shape=jax.ShapeDtypeStruct((m, n), BF16),
        grid=(m // tm, n // tn),
        in_specs=[
            pl.BlockSpec((tm, k), lambda i, j: (i, 0)),
            pl.BlockSpec((k, tn), lambda i, j: (0, j)),
            tab_spec, tab_spec, tab_spec,
        ],
        out_specs=pl.BlockSpec((tm, tn), lambda i, j: (i, j)),
        compiler_params=pltpu.CompilerParams(
            dimension_semantics=("parallel", "parallel"),
            vmem_limit_bytes=V7X_VMEM_LIMIT_BYTES),
        name="proj",
    )(x2d, w, *rope_tabs)


def _sb_logits_stage(q, k, strict):
    z = _nt_dot(q, k)
    sp = jnp.maximum(z, jnp.log2(1.0 + jnp.exp2(jnp.minimum(z, EXP2_ARG_MAX))))
    ls = z - sp
    if strict is not None:
        sp = jnp.where(strict, sp, 0.0)
    return ls, sp.astype(BF16), sp[:, :HEAD_DIM]


def _sb_weights_stage(ls, spb, sp0, v, u, acc_ref, carry_ref, rows, strict):
    groups = ls.shape[1] // HEAD_DIM
    excl = jnp.dot(spb, u, preferred_element_type=F32)
    carry = carry_ref[rows, :]
    a = jnp.exp2(ls - jnp.concatenate([carry] * groups, axis=1) - excl)
    if strict is not None:
        a = jnp.where(strict, a, 0.0)
    acc_ref[rows, :] += jnp.dot(a.astype(BF16), v, preferred_element_type=F32)
    carry_ref[rows, :] = carry + (excl[:, :1] + sp0[:, :1])


def _sb_kernel(q_ref, k_ref, v_ref, u_ref, o_ref, acc_ref, carry_ref, ls_ref, spb_ref, sp0_ref,
               *, blk, chunks):
    assert chunks % 2 == 0
    qi = pl.program_id(2)
    bq = blk * chunks
    u = u_ref[...]
    base = qi * chunks

    def k_tile(t):
        return k_ref[pl.ds(pl.multiple_of(t * blk, blk), blk), :]

    def v_tile(t):
        return v_ref[pl.ds(pl.multiple_of(t * blk, blk), blk), :]

    def tile_rows(t, r0, r1, diagonal, keep=None):
        rows = slice(r0, r1)
        strict = None
        if diagonal:
            row = lax.broadcasted_iota(jnp.int32, (r1 - r0, blk), 0)
            col = lax.broadcasted_iota(jnp.int32, (r1 - r0, blk), 1)
            strict = col < row
        if keep is not None:
            assert strict is None
            strict = jnp.broadcast_to(keep, (r1 - r0, blk))
        ls, spb, sp0 = _sb_logits_stage(q_ref[rows, :], k_tile(t), strict)
        _sb_weights_stage(ls, spb, sp0, v_tile(t), u, acc_ref, carry_ref, rows, strict)

    def min_carry(r0, r1):
        return jnp.min(carry_ref[r0:r1, :])

    acc_ref[...] = jnp.zeros(acc_ref.shape, F32)
    carry_ref[...] = jnp.zeros(carry_ref.shape, F32)

    for t in reversed(range(chunks)):
        tile_rows(base + t, t * blk, min((t + 2) * blk, bq), True)
    tile_rows(jnp.maximum(base - 1, 0), 0, blk, False, keep=base > 0)
    top = min_carry(0, blk)
    rest = min_carry(blk, bq)

    def rest_of_band():
        for t in reversed(range(chunks - 2)):
            tile_rows(base + t, (t + 2) * blk, bq, False)

        @pl.when(base > 0)
        def _():
            tile_rows(base - 1, blk, bq, False)

        return min_carry(blk, bq)

    rest = lax.cond(rest < SB_DEAD_CARRY_LOG2, rest_of_band, lambda: rest)

    def logits_to_slot(t, slot):
        ls_ref[slot], spb_ref[slot], sp0_ref[slot] = _sb_logits_stage(q_ref[...], k_tile(t), None)

    def position(p, par):
        _sb_weights_stage(ls_ref[par], spb_ref[par], sp0_ref[par], v_tile(base - 1 - p), u,
                          acc_ref, carry_ref, slice(None), None)
        logits_to_slot(jnp.maximum(base - 2 - p, 0), 1 - par)

    @pl.when(jnp.logical_and(base > 0, jnp.minimum(top, rest) < SB_DEAD_CARRY_LOG2))
    def _():
        logits_to_slot(base - 2, 1)
        for p in range(1, chunks):
            position(p, p % 2)

        def cond(state):
            g, go = state
            return jnp.logical_and(g < qi, go)

        def alive():
            return min_carry(0, bq) < SB_DEAD_CARRY_LOG2

        def body(state):
            g, _ = state
            for j in range(chunks):
                position(g * chunks + j, j % 2)
            return g + 1, alive()

        lax.while_loop(cond, body, (jnp.int32(1), alive()))

    o_ref[...] = acc_ref[...].astype(o_ref.dtype)


def _stick_breaking_attention(qkv, batch, seq, heads, *, blk=256, chunks=4):
    m = qkv.shape[0]
    bq = blk * chunks
    nq = seq // bq
    idx = jnp.arange(blk)
    u = (idx[:, None] > idx[None, :]).astype(BF16)
    return pl.pallas_call(
        functools.partial(_sb_kernel, blk=blk, chunks=chunks),
        out_shape=jax.ShapeDtypeStruct((m, heads * HEAD_DIM), BF16),
        grid=(batch, heads, nq),
        in_specs=[
            pl.BlockSpec((bq, HEAD_DIM), lambda b, h, i: (b * nq + i, h)),
            pl.BlockSpec((seq, HEAD_DIM), lambda b, h, i: (b, heads + h)),
            pl.BlockSpec((seq, HEAD_DIM), lambda b, h, i: (b, 2 * heads + h)),
            pl.BlockSpec((blk, blk), lambda b, h, i: (0, 0)),
        ],
        out_specs=pl.BlockSpec((bq, HEAD_DIM), lambda b, h, i: (b * nq + i, h)),
        scratch_shapes=[pltpu.VMEM((bq, HEAD_DIM), F32),
                        pltpu.VMEM((bq, HEAD_DIM), F32),
                        pltpu.VMEM((2, bq, blk), F32),
                        pltpu.VMEM((2, bq, blk), BF16),
                        pltpu.VMEM((2, bq, HEAD_DIM), F32)],
        compiler_params=pltpu.CompilerParams(
            dimension_semantics=("parallel", "parallel", "parallel"),
            vmem_limit_bytes=V7X_VMEM_LIMIT_BYTES),
        name="sb_attn",
    )(qkv, qkv, qkv, u)


def _diff_kernel(q_ref, k_ref, v_ref, lam_ref, g_ref, o_ref, m_ref, l_ref, acc_ref, kmax_ref, *,
                 blk, chunks, lambda_init):
    qi = pl.program_id(2)
    bq = blk * chunks
    wide = 2 * blk
    assert chunks == 4

    def scores(c, r0, k0, width, masked):
        lanes = slice(c * HEAD_DIM, (c + 1) * HEAD_DIM)
        s = _nt_dot(q_ref[r0:, lanes], k_ref[pl.ds(k0, width), lanes])
        if masked:
            row = lax.broadcasted_iota(jnp.int32, s.shape, 0)
            col = lax.broadcasted_iota(jnp.int32, s.shape, 1)
            s = jnp.where(col <= row, s, MASKED_LOGIT)
        return [s[:, g * HEAD_DIM:(g + 1) * HEAD_DIM] for g in range(width // HEAD_DIM)]

    def sweep(tile_fn):
        odd = qi & 1

        @pl.when(odd == 1)
        def _():
            for j in range(2):
                tile_fn(0, j * wide, wide, False)

        def body(i, carry):
            for j in range(4):
                tile_fn(0, pl.multiple_of((2 * odd + 4 * i + j) * wide, wide), wide, False)
            return carry

        lax.fori_loop(0, lax.shift_right_logical(qi, jnp.int32(1)), body, 0)
        for t in range(chunks):
            tile_fn(t * blk, pl.multiple_of((qi * chunks + t) * blk, blk), blk, True)

    def max_tile(r0, k0, width, masked):
        for c in range(2):
            m_ref[c, r0:, :] = functools.reduce(
                jnp.maximum, scores(c, r0, k0, width, masked), m_ref[c, r0:, :])

    def pv_tile(r0, k0, width, masked):
        v = v_ref[pl.ds(k0, width), :]
        for c in range(2):
            m = m_ref[c, r0:, :]
            p = [jnp.exp2(s - m) for s in scores(c, r0, k0, width, masked)]
            l_ref[c, r0:, :] += functools.reduce(jnp.add, p)
            acc_ref[c, r0:, :] += jnp.dot(jnp.concatenate(p, axis=1).astype(BF16), v,
                                          preferred_element_type=F32)

    half_r = lax.broadcasted_iota(jnp.int32, (2 * HEAD_DIM, 2 * HEAD_DIM), 0) >= HEAD_DIM
    half_c = lax.broadcasted_iota(jnp.int32, (2 * HEAD_DIM, 2 * HEAD_DIM), 1) >= HEAD_DIM
    ones_bd = jnp.where(half_r == half_c, 1.0, 0.0).astype(BF16)

    def sq_norms(x):
        return jnp.dot(x * x, ones_bd, preferred_element_type=F32)

    @pl.when(qi == 0)
    def _():
        def body(i, best):
            return jnp.maximum(
                best, sq_norms(k_ref[pl.ds(pl.multiple_of(i * wide, wide), wide), :]))

        best = lax.fori_loop(0, k_ref.shape[0] // wide, body,
                             jnp.zeros((wide, 2 * HEAD_DIM), F32))
        kmax_ref[...] = jnp.broadcast_to(jnp.max(best, axis=0, keepdims=True), kmax_ref.shape)

    bound = jnp.sqrt(sq_norms(q_ref[...]) * kmax_ref[:1, :])
    for c in range(2):
        m_ref[c] = bound[:, c * HEAD_DIM:(c + 1) * HEAD_DIM]
    bound_ok = 2.0 * jnp.max(bound) <= SHIFT_SLACK_LOG2

    @pl.when(jnp.logical_not(bound_ok))
    def _():
        m_ref[...] = jnp.full(m_ref.shape, MASKED_LOGIT, F32)
        sweep(max_tile)
        for c in range(2):
            m_ref[c] = jnp.broadcast_to(jnp.max(m_ref[c], axis=1, keepdims=True),
                                        m_ref.shape[1:])

    l_ref[...] = jnp.zeros(l_ref.shape, F32)
    acc_ref[...] = jnp.zeros(acc_ref.shape, F32)
    sweep(pv_tile)

    lam = lam_ref[...]
    lam_full = (jnp.exp(jnp.sum(lam[0:1] * lam[1:2], axis=1, keepdims=True))
                - jnp.exp(jnp.sum(lam[2:3] * lam[3:4], axis=1, keepdims=True)) + lambda_init)
    inv_l = [1.0 / jnp.sum(l_ref[c], axis=1, keepdims=True) for c in range(2)]
    o = acc_ref[0] * inv_l[0] - lam_full * (acc_ref[1] * inv_l[1])
    o = o * lax.rsqrt(jnp.mean(o * o, axis=-1, keepdims=True) + SUBLN_EPS)
    o_ref[...] = (o * (g_ref[...] * (1.0 - lambda_init))).astype(o_ref.dtype)


def _differential_attention(qkv, lam, subln_g, batch, seq, heads, lambda_init, *, blk=256,
                            chunks=4):
    m = qkv.shape[0]
    bq = blk * chunks
    nq = seq // bq
    w = DIFF_V_DIM
    return pl.pallas_call(
        functools.partial(_diff_kernel, blk=blk, chunks=chunks, lambda_init=lambda_init),
        out_shape=jax.ShapeDtypeStruct((m, heads * w), BF16),
        grid=(batch, heads, nq),
        in_specs=[
            pl.BlockSpec((bq, w), lambda b, h, i: (b * nq + i, h)),
            pl.BlockSpec((seq, w), lambda b, h, i: (b, heads + h)),
            pl.BlockSpec((seq, w), lambda b, h, i: (b, 2 * heads + h)),
            pl.BlockSpec((4, HEAD_DIM), lambda b, h, i: (0, 0)),
            pl.BlockSpec((1, w), lambda b, h, i: (0, 0)),
        ],
        out_specs=pl.BlockSpec((bq, w), lambda b, h, i: (b * nq + i, h)),
        scratch_shapes=[pltpu.VMEM((2, bq, HEAD_DIM), F32), pltpu.VMEM((2, bq, HEAD_DIM), F32),
                        pltpu.VMEM((2, bq, w), F32), pltpu.VMEM((8, 2 * HEAD_DIM), F32)],
        compiler_params=pltpu.CompilerParams(
            dimension_semantics=("parallel", "parallel", "arbitrary"),
            vmem_limit_bytes=V7X_VMEM_LIMIT_BYTES),
        name="diff_attn",
    )(qkv, qkv, qkv, lam, subln_g.reshape(1, w))


def _out_ln_kernel(a_ref, w_ref, x_ref, g_ref, b_ref, of_ref, ob_ref, *, alpha):
    for r0 in range(0, of_ref.shape[0], EPILOGUE_ROWS):
        rows = slice(r0, r0 + EPILOGUE_ROWS)
        h = jnp.dot(a_ref[rows, :], w_ref[...], preferred_element_type=F32)
        y = _layer_norm_rows(alpha * x_ref[rows, :] + h, g_ref[...], b_ref[...])
        of_ref[rows, :] = y
        ob_ref[rows, :] = y.astype(BF16)


def _out_proj_ln(attn, w, x, g, b, alpha, *, tm=512):
    m, k = attn.shape
    d = w.shape[1]
    row = pl.BlockSpec((tm, d), lambda i: (i, 0))
    vec = pl.BlockSpec((1, d), lambda i: (0, 0))
    return pl.pallas_call(
        functools.partial(_out_ln_kernel, alpha=alpha),
        out_shape=(jax.ShapeDtypeStruct((m, d), F32), jax.ShapeDtypeStruct((m, d), BF16)),
        grid=(m // tm,),
        in_specs=[pl.BlockSpec((tm, k), lambda i: (i, 0)),
                  pl.BlockSpec((k, d), lambda i: (0, 0)),
                  row, vec, vec],
        out_specs=(row, row),
        compiler_params=pltpu.CompilerParams(
            dimension_semantics=("parallel",),
            vmem_limit_bytes=V7X_VMEM_LIMIT_BYTES),
        name="out_proj_ln",
    )(attn, w, x, g.reshape(1, d), b.reshape(1, d))


def _mlp_ln_kernel(xb_ref, wu_ref, wd_ref, x_ref, g_ref, b_ref, of_ref, ob_ref, acc_ref, *, alpha):
    j = pl.program_id(1)

    @pl.when(j == 0)
    def _():
        acc_ref[...] = jnp.zeros_like(acc_ref)

    h = jnp.maximum(jnp.dot(xb_ref[...], wu_ref[...], preferred_element_type=F32), 0.0)
    acc_ref[...] += jnp.dot((h * h).astype(BF16), wd_ref[...], preferred_element_type=F32)

    @pl.when(j == pl.num_programs(1) - 1)
    def _():
        y = _layer_norm_rows(alpha * x_ref[...] + acc_ref[...], g_ref[...], b_ref[...])
        of_ref[...] = y
        ob_ref[...] = y.astype(BF16)


MLP_FF_TILE = 1024


def _mlp_ln(xb, w_up_all, w_down_all, layer, x, g, b, alpha, *, tm=512, tf=MLP_FF_TILE):
    m, d = xb.shape
    f = w_up_all.shape[2]
    row = lambda: pl.BlockSpec((tm, d), lambda i, j: (i, 0))
    vec = pl.BlockSpec((1, d), lambda i, j: (0, 0))
    return pl.pallas_call(
        functools.partial(_mlp_ln_kernel, alpha=alpha),
        out_shape=(jax.ShapeDtypeStruct((m, d), F32), jax.ShapeDtypeStruct((m, d), BF16)),
        grid=(m // tm, f // tf),
        in_specs=[row(),
                  pl.BlockSpec((None, d, tf), lambda i, j: (layer, 0, j)),
                  pl.BlockSpec((None, tf, d), lambda i, j: (layer, j, 0)),
                  row(), vec, vec],
        out_specs=(row(), row()),
        scratch_shapes=[pltpu.VMEM((tm, d), F32)],
        compiler_params=pltpu.CompilerParams(
            dimension_semantics=("parallel", "arbitrary"),
            vmem_limit_bytes=V7X_VMEM_LIMIT_BYTES),
        name="mlp_ln",
    )(xb, w_up_all, w_down_all, x, g.reshape(1, d), b.reshape(1, d))


def _rope_tables(seq):
    half = ROT_DIM // 2
    inv_freq = ROPE_THETA ** (-jnp.arange(0, ROT_DIM, 2, dtype=F32) / ROT_DIM)
    ang = jnp.arange(seq, dtype=F32)[:, None] * inv_freq[None, :]
    cos, sin = jnp.cos(ang), jnp.sin(ang)
    zeros = lambda n: jnp.zeros((seq, n), F32)
    c = jnp.concatenate([cos, cos, jnp.ones((seq, HEAD_DIM - ROT_DIM), F32)], axis=1)
    sa = jnp.concatenate([-sin, zeros(HEAD_DIM - half)], axis=1)
    sb = jnp.concatenate([zeros(half), sin, zeros(HEAD_DIM - ROT_DIM)], axis=1)
    return c, sa, sb


def kernel(x, ln_g, ln_b, sb_w_qkv, sb_w_o, kv_w, diff_w_q, diff_lambda, diff_subln_g,
           diff_w_o, mlp_w_up, mlp_w_down):
    batch, seq, d_model = x.shape
    depth = ln_g.shape[0]
    n_a = sb_w_qkv.shape[0]
    assert depth - n_a == 1
    alpha = (2 * depth) ** 0.25
    sb_heads = d_model // HEAD_DIM
    diff_heads = d_model // DIFF_V_DIM
    rope_tabs = _rope_tables(seq)

    xf = x.reshape(batch * seq, d_model)
    xb = xf
    kv_wb = kv_w.astype(BF16)
    w_up_b = mlp_w_up.astype(BF16)
    w_down_b = mlp_w_down.astype(BF16)
    for layer in range(depth):
        if layer < n_a:
            qkv = _project(xb, sb_w_qkv[layer].astype(BF16), rope_tabs, 0, d_model, seq)
            attn = _stick_breaking_attention(qkv, batch, seq, sb_heads)
            w_o = sb_w_o[layer]
        else:
            j = layer - n_a
            w_cat = jnp.concatenate([diff_w_q[j].astype(BF16), kv_wb], axis=1)
            qkv = _project(xb, w_cat, rope_tabs, 2 * d_model, d_model, seq)
            lambda_init = 0.8 - 0.6 * math.exp(-0.3 * layer)
            attn = _differential_attention(qkv, diff_lambda[j], diff_subln_g[j], batch, seq,
                                           diff_heads, lambda_init)
            w_o = diff_w_o[j]
        xf, xb = _out_proj_ln(attn, w_o.astype(BF16), xf, ln_g[layer, 0], ln_b[layer, 0], alpha)
        xf, xb = _mlp_ln(xb, w_up_b, w_down_b, layer, xf, ln_g[layer, 1], ln_b[layer, 1], alpha)
    return xf.reshape(batch, seq, d_model)
```

```python
import functools
import math

import jax
import jax.numpy as jnp
from jax import lax
from jax.experimental import pallas as pl
from jax.experimental.pallas import tpu as pltpu

HEAD_DIM = 128
DIFF_V_DIM = 256
ROT_DIM = HEAD_DIM // 4
ROPE_THETA = 500000.0
LN_EPS = 1e-5
SUBLN_EPS = 1e-5
LOGIT_SCALE_LOG2 = HEAD_DIM ** -0.5 * math.log2(math.e)
MASKED_LOGIT = -1e30
SHIFT_SLACK_LOG2 = 100.0
EXP2_ARG_MAX = 126.0
SB_DEAD_CARRY_LOG2 = 150.0

V7X_VMEM_LIMIT_BYTES = 56 * 1024 * 1024

F32 = jnp.float32
BF16 = jnp.bfloat16


def _nt_dot(a, b):
    return lax.dot_general(a, b, (((1,), (1,)), ((), ())), preferred_element_type=F32)


def _layer_norm_rows(y, g, b):
    mu = jnp.mean(y, axis=-1, keepdims=True)
    yc = y - mu
    var = jnp.mean(yc * yc, axis=-1, keepdims=True)
    return yc * lax.rsqrt(var + LN_EPS) * g + b


EPILOGUE_ROWS = 256


def _proj_kernel(x_ref, w_ref, c_ref, sa_ref, sb_ref, o_ref, *, rope_col_blocks, q_col_blocks,
                 q_scale):
    j = pl.program_id(1)
    tm, tn = o_ref.shape
    scale = jnp.where(j < q_col_blocks, q_scale, 1.0)

    def chunks(epilogue):
        for r0 in range(0, tm, EPILOGUE_ROWS):
            rows = slice(r0, r0 + EPILOGUE_ROWS)
            acc = jnp.dot(x_ref[rows, :].astype(BF16), w_ref[...], preferred_element_type=F32)
            epilogue(rows, acc * scale)

    def plain(rows, acc):
        o_ref[rows, :] = acc.astype(o_ref.dtype)

    def rotary(rows, acc):
        c = c_ref[rows, :]
        sa = sa_ref[rows, :]
        sb = sb_ref[rows, :]
        for g in range(tn // HEAD_DIM):
            t = acc[:, g * HEAD_DIM:(g + 1) * HEAD_DIM]
            rot = (t * c + pltpu.roll(t, HEAD_DIM - ROT_DIM // 2, 1) * sa
                   + pltpu.roll(t, ROT_DIM // 2, 1) * sb)
            o_ref[rows, g * HEAD_DIM:(g + 1) * HEAD_DIM] = rot.astype(o_ref.dtype)

    if rope_col_blocks == 0:
        chunks(plain)
        return

    @pl.when(j < rope_col_blocks)
    def _():
        chunks(rotary)

    @pl.when(j >= rope_col_blocks)
    def _():
        chunks(plain)


def _project(x2d, w, rope_tabs, rope_cols, q_cols, seq, *, tm=1024, tn=2048):
    m, k = x2d.shape
    n = w.shape[1]
    assert m % tm == 0 and n % tn == 0 and seq % tm == 0
    assert rope_cols % tn == 0 and q_cols % tn == 0
    row_blocks_per_seq = seq // tm
    tab_spec = pl.BlockSpec((tm, HEAD_DIM), lambda i, j: (i % row_blocks_per_seq, 0))
    return pl.pallas_call(
        functools.partial(_proj_kernel, rope_col_blocks=rope_cols // tn,
                          q_col_blocks=q_cols // tn, q_scale=LOGIT_SCALE_LOG2),
        out_shape=jax.ShapeDtypeStruct((m, n), BF16),
        grid=(m // tm, n // tn),
        in_specs=[
            pl.BlockSpec((tm, k), lambda i, j: (i, 0)),
            pl.BlockSpec((k, tn), lambda i, j: (0, j)),
            tab_spec, tab_spec, tab_spec,
        ],
        out_specs=pl.BlockSpec((tm, tn), lambda i, j: (i, j)),
        compiler_params=pltpu.CompilerParams(
            dimension_semantics=("parallel", "parallel"),
            vmem_limit_bytes=V7X_VMEM_LIMIT_BYTES),
        name="proj",
    )(x2d, w, *rope_tabs)


def _sb_logits_stage(q, k, strict):
    z = _nt_dot(q, k)
    sp = jnp.maximum(z, jnp.log2(1.0 + jnp.exp2(jnp.minimum(z, EXP2_ARG_MAX))))
    ls = z - sp
    if strict is not None:
        sp = jnp.where(strict, sp, 0.0)
    return ls, sp.astype(BF16), sp[:, :HEAD_DIM]


def _sb_weights_stage(ls, spb, sp0, v, u, acc_ref, carry_ref, rows, strict):
    groups = ls.shape[1] // HEAD_DIM
    excl = jnp.dot(spb, u, preferred_element_type=F32)
    carry = carry_ref[rows, :]
    a = jnp.exp2(ls - jnp.concatenate([carry] * groups, axis=1) - excl)
    if strict is not None:
        a = jnp.where(strict, a, 0.0)
    acc_ref[rows, :] += jnp.dot(a.astype(BF16), v, preferred_element_type=F32)
    carry_ref[rows, :] = carry + (excl[:, :1] + sp0[:, :1])


def _sb_kernel(q_ref, k_ref, v_ref, u_ref, o_ref, acc_ref, carry_ref, ls_ref, spb_ref, sp0_ref,
               *, blk, chunks):
    assert chunks % 2 == 0
    qi = pl.program_id(2)
    bq = blk * chunks
    u = u_ref[...]
    base = qi * chunks

    def k_tile(t):
        return k_ref[pl.ds(pl.multiple_of(t * blk, blk), blk), :]

    def v_tile(t):
        return v_ref[pl.ds(pl.multiple_of(t * blk, blk), blk), :]

    def tile_rows(t, r0, r1, diagonal, keep=None):
        rows = slice(r0, r1)
        strict = None
        if diagonal:
            row = lax.broadcasted_iota(jnp.int32, (r1 - r0, blk), 0)
            col = lax.broadcasted_iota(jnp.int32, (r1 - r0, blk), 1)
            strict = col < row
        if keep is not None:
            assert strict is None
            strict = jnp.broadcast_to(keep, (r1 - r0, blk))
        ls, spb, sp0 = _sb_logits_stage(q_ref[rows, :], k_tile(t), strict)
        _sb_weights_stage(ls, spb, sp0, v_tile(t), u, acc_ref, carry_ref, rows, strict)

    def min_carry(r0, r1):
        return jnp.min(carry_ref[r0:r1, :], axis=0)[0]

    acc_ref[...] = jnp.zeros(acc_ref.shape, F32)
    carry_ref[...] = jnp.zeros(carry_ref.shape, F32)

    for t in reversed(range(chunks)):
        tile_rows(base + t, t * blk, min((t + 2) * blk, bq), True)
    tile_rows(jnp.maximum(base - 1, 0), 0, blk, False, keep=base > 0)
    top = min_carry(0, blk)
    rest = min_carry(blk, bq)

    def rest_of_band():
        for t in reversed(range(chunks - 2)):
            tile_rows(base + t, (t + 2) * blk, bq, False)

        @pl.when(base > 0)
        def _():
            tile_rows(base - 1, blk, bq, False)

        return min_carry(blk, bq)

    rest = lax.cond(rest < SB_DEAD_CARRY_LOG2, rest_of_band, lambda: rest)

    def logits_to_slot(t, slot):
        ls_ref[slot], spb_ref[slot], sp0_ref[slot] = _sb_logits_stage(q_ref[...], k_tile(t), None)

    def position(p, par):
        _sb_weights_stage(ls_ref[par], spb_ref[par], sp0_ref[par], v_tile(base - 1 - p), u,
                          acc_ref, carry_ref, slice(None), None)
        logits_to_slot(jnp.maximum(base - 2 - p, 0), 1 - par)

    @pl.when(jnp.logical_and(base > 0, jnp.minimum(top, rest) < SB_DEAD_CARRY_LOG2))
    def _():
        logits_to_slot(base - 2, 1)
        for p in range(1, chunks):
            position(p, p % 2)

        def cond(state):
            g, go = state
            return jnp.logical_and(g < qi, go)

        def alive():
            return min_carry(0, bq) < SB_DEAD_CARRY_LOG2

        def body(state):
            g, _ = state
            for j in range(chunks):
                position(g * chunks + j, j % 2)
            return g + 1, alive()

        lax.while_loop(cond, body, (jnp.int32(1), alive()))

    o_ref[...] = acc_ref[...].astype(o_ref.dtype)


def _stick_breaking_attention(qkv, batch, seq, heads, *, blk=256, chunks=4):
    m = qkv.shape[0]
    bq = blk * chunks
    nq = seq // bq
    idx = jnp.arange(blk)
    u = (idx[:, None] > idx[None, :]).astype(BF16)
    return pl.pallas_call(
        functools.partial(_sb_kernel, blk=blk, chunks=chunks),
        out_shape=jax.ShapeDtypeStruct((m, heads * HEAD_DIM), BF16),
        grid=(batch, heads, nq),
        in_specs=[
            pl.BlockSpec((bq, HEAD_DIM), lambda b, h, i: (b * nq + i, h)),
            pl.BlockSpec((seq, HEAD_DIM), lambda b, h, i: (b, heads + h)),
            pl.BlockSpec((seq, HEAD_DIM), lambda b, h, i: (b, 2 * heads + h)),
            pl.BlockSpec((blk, blk), lambda b, h, i: (0, 0)),
        ],
        out_specs=pl.BlockSpec((bq, HEAD_DIM), lambda b, h, i: (b * nq + i, h)),
        scratch_shapes=[pltpu.VMEM((bq, HEAD_DIM), F32),
                        pltpu.VMEM((bq, HEAD_DIM), F32),
                        pltpu.VMEM((2, bq, blk), F32),
                        pltpu.VMEM((2, bq, blk), BF16),
                        pltpu.VMEM((2, bq, HEAD_DIM), F32)],
        compiler_params=pltpu.CompilerParams(
            dimension_semantics=("parallel", "parallel", "parallel"),
            vmem_limit_bytes=V7X_VMEM_LIMIT_BYTES),
        name="sb_attn",
    )(qkv, qkv, qkv, u)


def _diff_kernel(q_ref, k_ref, v_ref, lam_ref, g_ref, o_ref, m_ref, l_ref, acc_ref, kmax_ref, *,
                 blk, chunks, lambda_init):
    qi = pl.program_id(2)
    bq = blk * chunks
    wide = 2 * blk
    assert chunks == 4

    def scores(c, r0, k0, width, masked):
        lanes = slice(c * HEAD_DIM, (c + 1) * HEAD_DIM)
        s = _nt_dot(q_ref[r0:, lanes], k_ref[pl.ds(k0, width), lanes])
        if masked:
            row = lax.broadcasted_iota(jnp.int32, s.shape, 0)
            col = lax.broadcasted_iota(jnp.int32, s.shape, 1)
            s = jnp.where(col <= row, s, MASKED_LOGIT)
        return [s[:, g * HEAD_DIM:(g + 1) * HEAD_DIM] for g in range(width // HEAD_DIM)]

    def sweep(tile_fn):
        odd = qi & 1

        @pl.when(odd == 1)
        def _():
            for j in range(2):
                tile_fn(0, j * wide, wide, False)

        def body(i, carry):
            for j in range(4):
                tile_fn(0, pl.multiple_of((2 * odd + 4 * i + j) * wide, wide), wide, False)
            return carry

        lax.fori_loop(0, lax.shift_right_logical(qi, jnp.int32(1)), body, 0)
        for t in range(chunks):
            tile_fn(t * blk, pl.multiple_of((qi * chunks + t) * blk, blk), blk, True)

    def max_tile(r0, k0, width, masked):
        for c in range(2):
            m_ref[c, r0:, :] = functools.reduce(
                jnp.maximum, scores(c, r0, k0, width, masked), m_ref[c, r0:, :])

    def pv_tile(r0, k0, width, masked):
        v = v_ref[pl.ds(k0, width), :]
        for c in range(2):
            m = m_ref[c, r0:, :]
            p = [jnp.exp2(s - m) for s in scores(c, r0, k0, width, masked)]
            l_ref[c, r0:, :] += functools.reduce(jnp.add, p)
            acc_ref[c, r0:, :] += jnp.dot(jnp.concatenate(p, axis=1).astype(BF16), v,
                                          preferred_element_type=F32)

    half_r = lax.broadcasted_iota(jnp.int32, (2 * HEAD_DIM, 2 * HEAD_DIM), 0) >= HEAD_DIM
    half_c = lax.broadcasted_iota(jnp.int32, (2 * HEAD_DIM, 2 * HEAD_DIM), 1) >= HEAD_DIM
    ones_bd = jnp.where(half_r == half_c, 1.0, 0.0).astype(BF16)

    def sq_norms(x):
        return jnp.dot(x * x, ones_bd, preferred_element_type=F32)

    @pl.when(qi == 0)
    def _():
        def body(i, best):
            n = sq_norms(k_ref[pl.ds(pl.multiple_of(i * wide, wide), wide), :])
            return jnp.maximum(best, jnp.max(n.reshape(wide // 8, 8, 2 * HEAD_DIM), axis=0))

        best = lax.fori_loop(0, k_ref.shape[0] // wide, body, jnp.zeros(kmax_ref.shape, F32))
        kmax_ref[...] = jnp.broadcast_to(jnp.max(best, axis=0, keepdims=True), kmax_ref.shape)

    bound = jnp.sqrt(sq_norms(q_ref[...]) * kmax_ref[:1, :])
    for c in range(2):
        m_ref[c] = bound[:, c * HEAD_DIM:(c + 1) * HEAD_DIM]
    bound_ok = 2.0 * jnp.max(bound) <= SHIFT_SLACK_LOG2

    @pl.when(jnp.logical_not(bound_ok))
    def _():
        m_ref[...] = jnp.full(m_ref.shape, MASKED_LOGIT, F32)
        sweep(max_tile)
        for c in range(2):
            m_ref[c] = jnp.broadcast_to(jnp.max(m_ref[c], axis=1, keepdims=True),
                                        m_ref.shape[1:])

    l_ref[...] = jnp.zeros(l_ref.shape, F32)
    acc_ref[...] = jnp.zeros(acc_ref.shape, F32)
    sweep(pv_tile)

    lam = lam_ref[...]
    lam_full = (jnp.exp(jnp.sum(lam[0:1] * lam[1:2], axis=1, keepdims=True))
                - jnp.exp(jnp.sum(lam[2:3] * lam[3:4], axis=1, keepdims=True)) + lambda_init)
    inv_l = [1.0 / jnp.sum(l_ref[c], axis=1, keepdims=True) for c in range(2)]
    o = acc_ref[0] * inv_l[0] - lam_full * (acc_ref[1] * inv_l[1])
    o = o * lax.rsqrt(jnp.mean(o * o, axis=-1, keepdims=True) + SUBLN_EPS)
    o_ref[...] = (o * (g_ref[...] * (1.0 - lambda_init))).astype(o_ref.dtype)


def _differential_attention(qkv, lam, subln_g, batch, seq, heads, lambda_init, *, blk=256,
                            chunks=4):
    m = qkv.shape[0]
    bq = blk * chunks
    nq = seq // bq
    w = DIFF_V_DIM
    return pl.pallas_call(
        functools.partial(_diff_kernel, blk=blk, chunks=chunks, lambda_init=lambda_init),
        out_shape=jax.ShapeDtypeStruct((m, heads * w), BF16),
        grid=(batch, heads, nq),
        in_specs=[
            pl.BlockSpec((bq, w), lambda b, h, i: (b * nq + i, h)),
            pl.BlockSpec((seq, w), lambda b, h, i: (b, heads + h)),
            pl.BlockSpec((seq, w), lambda b, h, i: (b, 2 * heads + h)),
            pl.BlockSpec((4, HEAD_DIM), lambda b, h, i: (0, 0)),
            pl.BlockSpec((1, w), lambda b, h, i: (0, 0)),
        ],
        out_specs=pl.BlockSpec((bq, w), lambda b, h, i: (b * nq + i, h)),
        scratch_shapes=[pltpu.VMEM((2, bq, HEAD_DIM), F32), pltpu.VMEM((2, bq, HEAD_DIM), F32),
                        pltpu.VMEM((2, bq, w), F32), pltpu.VMEM((8, 2 * HEAD_DIM), F32)],
        compiler_params=pltpu.CompilerParams(
            dimension_semantics=("parallel", "parallel", "arbitrary"),
            vmem_limit_bytes=V7X_VMEM_LIMIT_BYTES),
        name="diff_attn",
    )(qkv, qkv, qkv, lam, subln_g.reshape(1, w))


def _out_ln_kernel(a_ref, w_ref, x_ref, g_ref, b_ref, of_ref, ob_ref, *, alpha):
    for r0 in range(0, of_ref.shape[0], EPILOGUE_ROWS):
        rows = slice(r0, r0 + EPILOGUE_ROWS)
        h = jnp.dot(a_ref[rows, :], w_ref[...], preferred_element_type=F32)
        y = _layer_norm_rows(alpha * x_ref[rows, :] + h, g_ref[...], b_ref[...])
        of_ref[rows, :] = y
        ob_ref[rows, :] = y.astype(BF16)


def _out_proj_ln(attn, w, x, g, b, alpha, *, tm=512):
    m, k = attn.shape
    d = w.shape[1]
    row = pl.BlockSpec((tm, d), lambda i: (i, 0))
    vec = pl.BlockSpec((1, d), lambda i: (0, 0))
    return pl.pallas_call(
        functools.partial(_out_ln_kernel, alpha=alpha),
        out_shape=(jax.ShapeDtypeStruct((m, d), F32), jax.ShapeDtypeStruct((m, d), BF16)),
        grid=(m // tm,),
        in_specs=[pl.BlockSpec((tm, k), lambda i: (i, 0)),
                  pl.BlockSpec((k, d), lambda i: (0, 0)),
                  row, vec, vec],
        out_specs=(row, row),
        compiler_params=pltpu.CompilerParams(
            dimension_semantics=("parallel",),
            vmem_limit_bytes=V7X_VMEM_LIMIT_BYTES),
        name="out_proj_ln",
    )(attn, w, x, g.reshape(1, d), b.reshape(1, d))


def _mlp_ln_kernel(xb_ref, wu_ref, wd_ref, x_ref, g_ref, b_ref, of_ref, ob_ref, acc_ref, *, alpha):
    j = pl.program_id(1)

    @pl.when(j == 0)
    def _():
        acc_ref[...] = jnp.zeros_like(acc_ref)

    h = jnp.maximum(jnp.dot(xb_ref[...], wu_ref[...], preferred_element_type=F32), 0.0)
    acc_ref[...] += jnp.dot((h * h).astype(BF16), wd_ref[...], preferred_element_type=F32)

    @pl.when(j == pl.num_programs(1) - 1)
    def _():
        y = _layer_norm_rows(alpha * x_ref[...] + acc_ref[...], g_ref[...], b_ref[...])
        of_ref[...] = y
        ob_ref[...] = y.astype(BF16)


MLP_FF_TILE = 1024


def _mlp_ln(xb, w_up_all, w_down_all, layer, x, g, b, alpha, *, tm=512, tf=MLP_FF_TILE):
    m, d = xb.shape
    f = w_up_all.shape[2]
    row = lambda: pl.BlockSpec((tm, d), lambda i, j: (i, 0))
    vec = pl.BlockSpec((1, d), lambda i, j: (0, 0))
    return pl.pallas_call(
        functools.partial(_mlp_ln_kernel, alpha=alpha),
        out_shape=(jax.ShapeDtypeStruct((m, d), F32), jax.ShapeDtypeStruct((m, d), BF16)),
        grid=(m // tm, f // tf),
        in_specs=[row(),
                  pl.BlockSpec((None, d, tf), lambda i, j: (layer, 0, j)),
                  pl.BlockSpec((None, tf, d), lambda i, j: (layer, j, 0)),
                  row(), vec, vec],
        out_specs=(row(), row()),
        scratch_shapes=[pltpu.VMEM((tm, d), F32)],
        compiler_params=pltpu.CompilerParams(
            dimension_semantics=("parallel", "arbitrary"),
            vmem_limit_bytes=V7X_VMEM_LIMIT_BYTES),
        name="mlp_ln",
    )(xb, w_up_all, w_down_all, x, g.reshape(1, d), b.reshape(1, d))


def _rope_tables(seq):
    half = ROT_DIM // 2
    inv_freq = ROPE_THETA ** (-jnp.arange(0, ROT_DIM, 2, dtype=F32) / ROT_DIM)
    ang = jnp.arange(seq, dtype=F32)[:, None] * inv_freq[None, :]
    cos, sin = jnp.cos(ang), jnp.sin(ang)
    zeros = lambda n: jnp.zeros((seq, n), F32)
    c = jnp.concatenate([cos, cos, jnp.ones((seq, HEAD_DIM - ROT_DIM), F32)], axis=1)
    sa = jnp.concatenate([-sin, zeros(HEAD_DIM - half)], axis=1)
    sb = jnp.concatenate([zeros(half), sin, zeros(HEAD_DIM - ROT_DIM)], axis=1)
    return c, sa, sb


def kernel(x, ln_g, ln_b, sb_w_qkv, sb_w_o, kv_w, diff_w_q, diff_lambda, diff_subln_g,
           diff_w_o, mlp_w_up, mlp_w_down):
    batch, seq, d_model = x.shape
    depth = ln_g.shape[0]
    n_a = sb_w_qkv.shape[0]
    assert depth - n_a == 1
    alpha = (2 * depth) ** 0.25
    sb_heads = d_model // HEAD_DIM
    diff_heads = d_model // DIFF_V_DIM
    rope_tabs = _rope_tables(seq)

    xf = x.reshape(batch * seq, d_model)
    xb = xf
    kv_wb = kv_w.astype(BF16)
    w_up_b = mlp_w_up.astype(BF16)
    w_down_b = mlp_w_down.astype(BF16)
    for layer in range(depth):
        if layer < n_a:
            qkv = _project(xb, sb_w_qkv[layer].astype(BF16), rope_tabs, 0, d_model, seq)
            attn = _stick_breaking_attention(qkv, batch, seq, sb_heads)
            w_o = sb_w_o[layer]
        else:
            j = layer - n_a
            w_cat = jnp.concatenate([diff_w_q[j].astype(BF16), kv_wb], axis=1)
            qkv = _project(xb, w_cat, rope_tabs, 2 * d_model, d_model, seq)
            lambda_init = 0.8 - 0.6 * math.exp(-0.3 * layer)
            attn = _differential_attention(qkv, diff_lambda[j], diff_subln_g[j], batch, seq,
                                           diff_heads, lambda_init)
            w_o = diff_w_o[j]
        xf, xb = _out_proj_ln(attn, w_o.astype(BF16), xf, ln_g[layer, 0], ln_b[layer, 0], alpha)
        xf, xb = _mlp_ln(xb, w_up_b, w_down_b, layer, xf, ln_g[layer, 1], ln_b[layer, 1], alpha)
    return xf.reshape(batch, seq, d_model)
```

```python
import functools
import math

import jax
import jax.numpy as jnp
from jax import lax
from jax.experimental import pallas as pl
from jax.experimental.pallas import tpu as pltpu

HEAD_DIM = 128
DIFF_V_DIM = 256
ROT_DIM = HEAD_DIM // 4
ROPE_THETA = 500000.0
LN_EPS = 1e-5
SUBLN_EPS = 1e-5
LOGIT_SCALE_LOG2 = HEAD_DIM ** -0.5 * math.log2(math.e)
MASKED_LOGIT = -1e30
SHIFT_SLACK_LOG2 = 100.0
EXP2_ARG_MAX = 126.0
SB_DEAD_CARRY_LOG2 = 150.0

V7X_VMEM_LIMIT_BYTES = 56 * 1024 * 1024

F32 = jnp.float32
BF16 = jnp.bfloat16


def _nt_dot(a, b):
    return lax.dot_general(a, b, (((1,), (1,)), ((), ())), preferred_element_type=F32)


def _layer_norm_rows(y, g, b):
    mu = jnp.mean(y, axis=-1, keepdims=True)
    yc = y - mu
    var = jnp.mean(yc * yc, axis=-1, keepdims=True)
    return yc * lax.rsqrt(var + LN_EPS) * g + b


EPILOGUE_ROWS = 256


def _proj_kernel(x_ref, w_ref, c_ref, sa_ref, sb_ref, o_ref, *, rope_col_blocks, q_col_blocks,
                 q_scale):
    j = pl.program_id(1)
    tm, tn = o_ref.shape
    scale = jnp.where(j < q_col_blocks, q_scale, 1.0)

    def chunks(epilogue):
        for r0 in range(0, tm, EPILOGUE_ROWS):
            rows = slice(r0, r0 + EPILOGUE_ROWS)
            acc = jnp.dot(x_ref[rows, :].astype(BF16), w_ref[...], preferred_element_type=F32)
            epilogue(rows, acc * scale)

    def plain(rows, acc):
        o_ref[rows, :] = acc.astype(o_ref.dtype)

    def rotary(rows, acc):
        c = c_ref[rows, :]
        sa = sa_ref[rows, :]
        sb = sb_ref[rows, :]
        for g in range(tn // HEAD_DIM):
            t = acc[:, g * HEAD_DIM:(g + 1) * HEAD_DIM]
            rot = (t * c + pltpu.roll(t, HEAD_DIM - ROT_DIM // 2, 1) * sa
                   + pltpu.roll(t, ROT_DIM // 2, 1) * sb)
            o_ref[rows, g * HEAD_DIM:(g + 1) * HEAD_DIM] = rot.astype(o_ref.dtype)

    if rope_col_blocks == 0:
        chunks(plain)
        return

    @pl.when(j < rope_col_blocks)
    def _():
        chunks(rotary)

    @pl.when(j >= rope_col_blocks)
    def _():
        chunks(plain)


def _project(x2d, w, rope_tabs, rope_cols, q_cols, seq, *, tm=1024, tn=2048):
    m, k = x2d.shape
    n = w.shape[1]
    assert m % tm == 0 and n % tn == 0 and seq % tm == 0
    assert rope_cols % tn == 0 and q_cols % tn == 0
    row_blocks_per_seq = seq // tm
    tab_spec = pl.BlockSpec((tm, HEAD_DIM), lambda i, j: (i % row_blocks_per_seq, 0))
    return pl.pallas_call(
        functools.partial(_proj_kernel, rope_col_blocks=rope_cols // tn,
                          q_col_blocks=q_cols // tn, q_scale=LOGIT_SCALE_LOG2),
        out_shape=jax.ShapeDtypeStruct((m, n), BF16),
        grid=(m // tm, n // tn),
        in_specs=[
            pl.BlockSpec((tm, k), lambda i, j: (i, 0)),
            pl.BlockSpec((k, tn), lambda i, j: (0, j)),
            tab_spec, tab_spec, tab_spec,
        ],
        out_specs=pl.BlockSpec((tm, tn), lambda i, j: (i, j)),
        compiler_params=pltpu.CompilerParams(
            dimension_semantics=("parallel", "parallel"),
            vmem_limit_bytes=V7X_VMEM_LIMIT_BYTES),
        name="proj",
    )(x2d, w, *rope_tabs)


def _sb_logits_stage(q, k, strict):
    z = _nt_dot(q, k)
    sp = jnp.maximum(z, jnp.log2(1.0 + jnp.exp2(jnp.minimum(z, EXP2_ARG_MAX))))
    ls = z - sp
    if strict is not None:
        sp = jnp.where(strict, sp, 0.0)
    return ls, sp.astype(BF16), sp[:, :HEAD_DIM]


def _sb_weights_stage(ls, spb, sp0, v, u, acc_ref, carry_ref, rows, strict):
    groups = ls.shape[1] // HEAD_DIM
    excl = jnp.dot(spb, u, preferred_element_type=F32)
    carry = carry_ref[rows, :]
    a = jnp.exp2(ls - jnp.concatenate([carry] * groups, axis=1) - excl)
    if strict is not None:
        a = jnp.where(strict, a, 0.0)
    acc_ref[rows, :] += jnp.dot(a.astype(BF16), v, preferred_element_type=F32)
    carry_ref[rows, :] = carry + (excl[:, :1] + sp0[:, :1])


def _sb_kernel(q_ref, k_ref, v_ref, u_ref, o_ref, acc_ref, carry_ref, ls_ref, spb_ref, sp0_ref,
               *, blk, chunks):
    assert chunks % 2 == 0
    qi = pl.program_id(2)
    bq = blk * chunks
    u = u_ref[...]
    base = qi * chunks

    def k_tile(t):
        return k_ref[pl.ds(pl.multiple_of(t * blk, blk), blk), :]

    def v_tile(t):
        return v_ref[pl.ds(pl.multiple_of(t * blk, blk), blk), :]

    def tile_rows(t, r0, r1, diagonal, keep=None):
        rows = slice(r0, r1)
        strict = None
        if diagonal:
            row = lax.broadcasted_iota(jnp.int32, (r1 - r0, blk), 0)
            col = lax.broadcasted_iota(jnp.int32, (r1 - r0, blk), 1)
            strict = col < row
        if keep is not None:
            assert strict is None
            strict = jnp.broadcast_to(keep, (r1 - r0, blk))
        ls, spb, sp0 = _sb_logits_stage(q_ref[rows, :], k_tile(t), strict)
        _sb_weights_stage(ls, spb, sp0, v_tile(t), u, acc_ref, carry_ref, rows, strict)

    def min_carry(r0, r1):
        return jnp.min(carry_ref[r0:r1, :], axis=0)[0]

    acc_ref[...] = jnp.zeros(acc_ref.shape, F32)
    carry_ref[...] = jnp.zeros(carry_ref.shape, F32)

    for t in reversed(range(chunks)):
        tile_rows(base + t, t * blk, min((t + 2) * blk, bq), True)
    tile_rows(jnp.maximum(base - 1, 0), 0, blk, False, keep=base > 0)
    top = min_carry(0, blk)
    rest = min_carry(blk, bq)

    def rest_of_band():
        for t in reversed(range(chunks - 2)):
            tile_rows(base + t, (t + 2) * blk, bq, False)

        @pl.when(base > 0)
        def _():
            tile_rows(base - 1, blk, bq, False)

        return min_carry(blk, bq)

    rest = lax.cond(rest < SB_DEAD_CARRY_LOG2, rest_of_band, lambda: rest)

    def logits_to_slot(t, slot):
        ls_ref[slot], spb_ref[slot], sp0_ref[slot] = _sb_logits_stage(q_ref[...], k_tile(t), None)

    def position(p, par):
        _sb_weights_stage(ls_ref[par], spb_ref[par], sp0_ref[par], v_tile(base - 1 - p), u,
                          acc_ref, carry_ref, slice(None), None)
        logits_to_slot(jnp.maximum(base - 2 - p, 0), 1 - par)

    @pl.when(jnp.logical_and(base > 0, jnp.minimum(top, rest) < SB_DEAD_CARRY_LOG2))
    def _():
        logits_to_slot(base - 2, 1)
        for p in range(1, chunks):
            position(p, p % 2)

        def cond(state):
            g, go = state
            return jnp.logical_and(g < qi, go)

        def alive():
            return min_carry(0, bq) < SB_DEAD_CARRY_LOG2

        def body(state):
            g, _ = state
            for j in range(chunks):
                position(g * chunks + j, j % 2)
            return g + 1, alive()

        lax.while_loop(cond, body, (jnp.int32(1), alive()))

    o_ref[...] = acc_ref[...].astype(o_ref.dtype)


def _stick_breaking_attention(qkv, batch, seq, heads, *, blk=256, chunks=8):
    m = qkv.shape[0]
    bq = blk * chunks
    nq = seq // bq
    idx = jnp.arange(blk)
    u = (idx[:, None] > idx[None, :]).astype(BF16)
    return pl.pallas_call(
        functools.partial(_sb_kernel, blk=blk, chunks=chunks),
        out_shape=jax.ShapeDtypeStruct((m, heads * HEAD_DIM), BF16),
        grid=(batch, heads, nq),
        in_specs=[
            pl.BlockSpec((bq, HEAD_DIM), lambda b, h, i: (b * nq + i, h)),
            pl.BlockSpec((seq, HEAD_DIM), lambda b, h, i: (b, heads + h)),
            pl.BlockSpec((seq, HEAD_DIM), lambda b, h, i: (b, 2 * heads + h)),
            pl.BlockSpec((blk, blk), lambda b, h, i: (0, 0)),
        ],
        out_specs=pl.BlockSpec((bq, HEAD_DIM), lambda b, h, i: (b * nq + i, h)),
        scratch_shapes=[pltpu.VMEM((bq, HEAD_DIM), F32),
                        pltpu.VMEM((bq, HEAD_DIM), F32),
                        pltpu.VMEM((2, bq, blk), F32),
                        pltpu.VMEM((2, bq, blk), BF16),
                        pltpu.VMEM((2, bq, HEAD_DIM), F32)],
        compiler_params=pltpu.CompilerParams(
            dimension_semantics=("parallel", "parallel", "parallel"),
            vmem_limit_bytes=V7X_VMEM_LIMIT_BYTES),
        name="sb_attn",
    )(qkv, qkv, qkv, u)


def _diff_kernel(q_ref, k_ref, v_ref, lam_ref, g_ref, o_ref, m_ref, l_ref, acc_ref, kmax_ref, *,
                 blk, chunks, lambda_init):
    qi = pl.program_id(2)
    bq = blk * chunks
    wide = 2 * blk
    assert chunks == 4

    def scores(c, r0, k0, width, masked):
        lanes = slice(c * HEAD_DIM, (c + 1) * HEAD_DIM)
        s = _nt_dot(q_ref[r0:, lanes], k_ref[pl.ds(k0, width), lanes])
        if masked:
            row = lax.broadcasted_iota(jnp.int32, s.shape, 0)
            col = lax.broadcasted_iota(jnp.int32, s.shape, 1)
            s = jnp.where(col <= row, s, MASKED_LOGIT)
        return [s[:, g * HEAD_DIM:(g + 1) * HEAD_DIM] for g in range(width // HEAD_DIM)]

    def sweep(tile_fn):
        odd = qi & 1

        @pl.when(odd == 1)
        def _():
            for j in range(2):
                tile_fn(0, j * wide, wide, False)

        def body(i, carry):
            for j in range(4):
                tile_fn(0, pl.multiple_of((2 * odd + 4 * i + j) * wide, wide), wide, False)
            return carry

        lax.fori_loop(0, lax.shift_right_logical(qi, jnp.int32(1)), body, 0)
        for t in range(chunks):
            tile_fn(t * blk, pl.multiple_of((qi * chunks + t) * blk, blk), blk, True)

    def max_tile(r0, k0, width, masked):
        for c in range(2):
            m_ref[c, r0:, :] = functools.reduce(
                jnp.maximum, scores(c, r0, k0, width, masked), m_ref[c, r0:, :])

    def pv_tile(r0, k0, width, masked):
        v = v_ref[pl.ds(k0, width), :]
        for c in range(2):
            m = m_ref[c, r0:, :]
            p = [jnp.exp2(s - m) for s in scores(c, r0, k0, width, masked)]
            l_ref[c, r0:, :] += functools.reduce(jnp.add, p)
            acc_ref[c, r0:, :] += jnp.dot(jnp.concatenate(p, axis=1).astype(BF16), v,
                                          preferred_element_type=F32)

    half_r = lax.broadcasted_iota(jnp.int32, (2 * HEAD_DIM, 2 * HEAD_DIM), 0) >= HEAD_DIM
    half_c = lax.broadcasted_iota(jnp.int32, (2 * HEAD_DIM, 2 * HEAD_DIM), 1) >= HEAD_DIM
    ones_bd = jnp.where(half_r == half_c, 1.0, 0.0).astype(BF16)

    def sq_norms(x):
        return jnp.dot(x * x, ones_bd, preferred_element_type=F32)

    @pl.when(qi == 0)
    def _():
        def body(i, best):
            n = sq_norms(k_ref[pl.ds(pl.multiple_of(i * wide, wide), wide), :])
            return jnp.maximum(best, jnp.max(n.reshape(wide // 8, 8, 2 * HEAD_DIM), axis=0))

        best = lax.fori_loop(0, k_ref.shape[0] // wide, body, jnp.zeros(kmax_ref.shape, F32))
        kmax_ref[...] = jnp.broadcast_to(jnp.max(best, axis=0, keepdims=True), kmax_ref.shape)

    bound = jnp.sqrt(sq_norms(q_ref[...]) * kmax_ref[:1, :])
    for c in range(2):
        m_ref[c] = bound[:, c * HEAD_DIM:(c + 1) * HEAD_DIM]
    bound_ok = 2.0 * jnp.max(bound) <= SHIFT_SLACK_LOG2

    @pl.when(jnp.logical_not(bound_ok))
    def _():
        m_ref[...] = jnp.full(m_ref.shape, MASKED_LOGIT, F32)
        sweep(max_tile)
        for c in range(2):
            m_ref[c] = jnp.broadcast_to(jnp.max(m_ref[c], axis=1, keepdims=True),
                                        m_ref.shape[1:])

    l_ref[...] = jnp.zeros(l_ref.shape, F32)
    acc_ref[...] = jnp.zeros(acc_ref.shape, F32)
    sweep(pv_tile)

    lam = lam_ref[...]
    lam_full = (jnp.exp(jnp.sum(lam[0:1] * lam[1:2], axis=1, keepdims=True))
                - jnp.exp(jnp.sum(lam[2:3] * lam[3:4], axis=1, keepdims=True)) + lambda_init)
    inv_l = [1.0 / jnp.sum(l_ref[c], axis=1, keepdims=True) for c in range(2)]
    o = acc_ref[0] * inv_l[0] - lam_full * (acc_ref[1] * inv_l[1])
    o = o * lax.rsqrt(jnp.mean(o * o, axis=-1, keepdims=True) + SUBLN_EPS)
    o_ref[...] = (o * (g_ref[...] * (1.0 - lambda_init))).astype(o_ref.dtype)


def _differential_attention(qkv, lam, subln_g, batch, seq, heads, lambda_init, *, blk=256,
                            chunks=4):
    m = qkv.shape[0]
    bq = blk * chunks
    nq = seq // bq
    w = DIFF_V_DIM
    return pl.pallas_call(
        functools.partial(_diff_kernel, blk=blk, chunks=chunks, lambda_init=lambda_init),
        out_shape=jax.ShapeDtypeStruct((m, heads * w), BF16),
        grid=(batch, heads, nq),
        in_specs=[
            pl.BlockSpec((bq, w), lambda b, h, i: (b * nq + i, h)),
            pl.BlockSpec((seq, w), lambda b, h, i: (b, heads + h)),
            pl.BlockSpec((seq, w), lambda b, h, i: (b, 2 * heads + h)),
            pl.BlockSpec((4, HEAD_DIM), lambda b, h, i: (0, 0)),
            pl.BlockSpec((1, w), lambda b, h, i: (0, 0)),
        ],
        out_specs=pl.BlockSpec((bq, w), lambda b, h, i: (b * nq + i, h)),
        scratch_shapes=[pltpu.VMEM((2, bq, HEAD_DIM), F32), pltpu.VMEM((2, bq, HEAD_DIM), F32),
                        pltpu.VMEM((2, bq, w), F32), pltpu.VMEM((8, 2 * HEAD_DIM), F32)],
        compiler_params=pltpu.CompilerParams(
            dimension_semantics=("parallel", "parallel", "arbitrary"),
            vmem_limit_bytes=V7X_VMEM_LIMIT_BYTES),
        name="diff_attn",
    )(qkv, qkv, qkv, lam, subln_g.reshape(1, w))


def _out_ln_kernel(a_ref, w_ref, x_ref, g_ref, b_ref, of_ref, ob_ref, *, alpha):
    for r0 in range(0, of_ref.shape[0], EPILOGUE_ROWS):
        rows = slice(r0, r0 + EPILOGUE_ROWS)
        h = jnp.dot(a_ref[rows, :], w_ref[...], preferred_element_type=F32)
        y = _layer_norm_rows(alpha * x_ref[rows, :] + h, g_ref[...], b_ref[...])
        of_ref[rows, :] = y
        ob_ref[rows, :] = y.astype(BF16)


def _out_proj_ln(attn, w, x, g, b, alpha, *, tm=512):
    m, k = attn.shape
    d = w.shape[1]
    row = pl.BlockSpec((tm, d), lambda i: (i, 0))
    vec = pl.BlockSpec((1, d), lambda i: (0, 0))
    return pl.pallas_call(
        functools.partial(_out_ln_kernel, alpha=alpha),
        out_shape=(jax.ShapeDtypeStruct((m, d), F32), jax.ShapeDtypeStruct((m, d), BF16)),
        grid=(m // tm,),
        in_specs=[pl.BlockSpec((tm, k), lambda i: (i, 0)),
                  pl.BlockSpec((k, d), lambda i: (0, 0)),
                  row, vec, vec],
        out_specs=(row, row),
        compiler_params=pltpu.CompilerParams(
            dimension_semantics=("parallel",),
            vmem_limit_bytes=V7X_VMEM_LIMIT_BYTES),
        name="out_proj_ln",
    )(attn, w, x, g.reshape(1, d), b.reshape(1, d))


def _mlp_ln_kernel(xb_ref, wu_ref, wd_ref, x_ref, g_ref, b_ref, of_ref, ob_ref, acc_ref, *, alpha):
    j = pl.program_id(1)

    @pl.when(j == 0)
    def _():
        acc_ref[...] = jnp.zeros_like(acc_ref)

    h = jnp.maximum(jnp.dot(xb_ref[...], wu_ref[...], preferred_element_type=F32), 0.0)
    acc_ref[...] += jnp.dot((h * h).astype(BF16), wd_ref[...], preferred_element_type=F32)

    @pl.when(j == pl.num_programs(1) - 1)
    def _():
        y = _layer_norm_rows(alpha * x_ref[...] + acc_ref[...], g_ref[...], b_ref[...])
        of_ref[...] = y
        ob_ref[...] = y.astype(BF16)


MLP_FF_TILE = 1024


def _mlp_ln(xb, w_up_all, w_down_all, layer, x, g, b, alpha, *, tm=512, tf=MLP_FF_TILE):
    m, d = xb.shape
    f = w_up_all.shape[2]
    row = lambda: pl.BlockSpec((tm, d), lambda i, j: (i, 0))
    vec = pl.BlockSpec((1, d), lambda i, j: (0, 0))
    return pl.pallas_call(
        functools.partial(_mlp_ln_kernel, alpha=alpha),
        out_shape=(jax.ShapeDtypeStruct((m, d), F32), jax.ShapeDtypeStruct((m, d), BF16)),
        grid=(m // tm, f // tf),
        in_specs=[row(),
                  pl.BlockSpec((None, d, tf), lambda i, j: (layer, 0, j)),
                  pl.BlockSpec((None, tf, d), lambda i, j: (layer, j, 0)),
                  row(), vec, vec],
        out_specs=(row(), row()),
        scratch_shapes=[pltpu.VMEM((tm, d), F32)],
        compiler_params=pltpu.CompilerParams(
            dimension_semantics=("parallel", "arbitrary"),
            vmem_limit_bytes=V7X_VMEM_LIMIT_BYTES),
        name="mlp_ln",
    )(xb, w_up_all, w_down_all, x, g.reshape(1, d), b.reshape(1, d))


def _rope_tables(seq):
    half = ROT_DIM // 2
    inv_freq = ROPE_THETA ** (-jnp.arange(0, ROT_DIM, 2, dtype=F32) / ROT_DIM)
    ang = jnp.arange(seq, dtype=F32)[:, None] * inv_freq[None, :]
    cos, sin = jnp.cos(ang), jnp.sin(ang)
    zeros = lambda n: jnp.zeros((seq, n), F32)
    c = jnp.concatenate([cos, cos, jnp.ones((seq, HEAD_DIM - ROT_DIM), F32)], axis=1)
    sa = jnp.concatenate([-sin, zeros(HEAD_DIM - half)], axis=1)
    sb = jnp.concatenate([zeros(half), sin, zeros(HEAD_DIM - ROT_DIM)], axis=1)
    return c, sa, sb


def kernel(x, ln_g, ln_b, sb_w_qkv, sb_w_o, kv_w, diff_w_q, diff_lambda, diff_subln_g,
           diff_w_o, mlp_w_up, mlp_w_down):
    batch, seq, d_model = x.shape
    depth = ln_g.shape[0]
    n_a = sb_w_qkv.shape[0]
    assert depth - n_a == 1
    alpha = (2 * depth) ** 0.25
    sb_heads = d_model // HEAD_DIM
    diff_heads = d_model // DIFF_V_DIM
    rope_tabs = _rope_tables(seq)

    xf = x.reshape(batch * seq, d_model)
    xb = xf
    kv_wb = kv_w.astype(BF16)
    w_up_b = mlp_w_up.astype(BF16)
    w_down_b = mlp_w_down.astype(BF16)
    for layer in range(depth):
        if layer < n_a:
            qkv = _project(xb, sb_w_qkv[layer].astype(BF16), rope_tabs, 0, d_model, seq)
            attn = _stick_breaking_attention(qkv, batch, seq, sb_heads)
            w_o = sb_w_o[layer]
        else:
            j = layer - n_a
            w_cat = jnp.concatenate([diff_w_q[j].astype(BF16), kv_wb], axis=1)
            qkv = _project(xb, w_cat, rope_tabs, 2 * d_model, d_model, seq)
            lambda_init = 0.8 - 0.6 * math.exp(-0.3 * layer)
            attn = _differential_attention(qkv, diff_lambda[j], diff_subln_g[j], batch, seq,
                                           diff_heads, lambda_init)
            w_o = diff_w_o[j]
        xf, xb = _out_proj_ln(attn, w_o.astype(BF16), xf, ln_g[layer, 0], ln_b[layer, 0], alpha)
        xf, xb = _mlp_ln(xb, w_up_b, w_down_b, layer, xf, ln_g[layer, 1], ln_b[layer, 1], alpha)
    return xf.reshape(batch, seq, d_model)
```

```python
import functools
import math

import jax
import jax.numpy as jnp
from jax import lax
from jax.experimental import pallas as pl
from jax.experimental.pallas import tpu as pltpu

HEAD_DIM = 128
DIFF_V_DIM = 256
ROT_DIM = HEAD_DIM // 4
ROPE_THETA = 500000.0
LN_EPS = 1e-5
SUBLN_EPS = 1e-5
LOGIT_SCALE_LOG2 = HEAD_DIM ** -0.5 * math.log2(math.e)
MASKED_LOGIT = -1e30
SHIFT_SLACK_LOG2 = 100.0
EXP2_ARG_MAX = 126.0
SB_DEAD_CARRY_LOG2 = 150.0

V7X_VMEM_LIMIT_BYTES = 56 * 1024 * 1024

F32 = jnp.float32
BF16 = jnp.bfloat16


def _nt_dot(a, b):
    return lax.dot_general(a, b, (((1,), (1,)), ((), ())), preferred_element_type=F32)


def _layer_norm_rows(y, g, b):
    mu = jnp.mean(y, axis=-1, keepdims=True)
    yc = y - mu
    var = jnp.mean(yc * yc, axis=-1, keepdims=True)
    return yc * lax.rsqrt(var + LN_EPS) * g + b


EPILOGUE_ROWS = 256


def _proj_kernel(x_ref, w_ref, c_ref, sa_ref, sb_ref, o_ref, *, rope_col_blocks, q_col_blocks,
                 q_scale):
    j = pl.program_id(1)
    tm, tn = o_ref.shape
    scale = jnp.where(j < q_col_blocks, q_scale, 1.0)

    def chunks(epilogue):
        for r0 in range(0, tm, EPILOGUE_ROWS):
            rows = slice(r0, r0 + EPILOGUE_ROWS)
            acc = jnp.dot(x_ref[rows, :].astype(BF16), w_ref[...], preferred_element_type=F32)
            epilogue(rows, acc * scale)

    def plain(rows, acc):
        o_ref[rows, :] = acc.astype(o_ref.dtype)

    def rotary(rows, acc):
        c = c_ref[rows, :]
        sa = sa_ref[rows, :]
        sb = sb_ref[rows, :]
        for g in range(tn // HEAD_DIM):
            t = acc[:, g * HEAD_DIM:(g + 1) * HEAD_DIM]
            rot = (t * c + pltpu.roll(t, HEAD_DIM - ROT_DIM // 2, 1) * sa
                   + pltpu.roll(t, ROT_DIM // 2, 1) * sb)
            o_ref[rows, g * HEAD_DIM:(g + 1) * HEAD_DIM] = rot.astype(o_ref.dtype)

    if rope_col_blocks == 0:
        chunks(plain)
        return

    @pl.when(j < rope_col_blocks)
    def _():
        chunks(rotary)

    @pl.when(j >= rope_col_blocks)
    def _():
        chunks(plain)


def _project(x2d, w, rope_tabs, rope_cols, q_cols, seq, *, tm=1024, tn=2048):
    m, k = x2d.shape
    n = w.shape[1]
    assert m % tm == 0 and n % tn == 0 and seq % tm == 0
    assert rope_cols % tn == 0 and q_cols % tn == 0
    row_blocks_per_seq = seq // tm
    tab_spec = pl.BlockSpec((tm, HEAD_DIM), lambda i, j: (i % row_blocks_per_seq, 0))
    return pl.pallas_call(
        functools.partial(_proj_kernel, rope_col_blocks=rope_cols // tn,
                          q_col_blocks=q_cols // tn, q_scale=LOGIT_SCALE_LOG2),
        out_shape=jax.ShapeDtypeStruct((m, n), BF16),
        grid=(m // tm, n // tn),
        in_specs=[
            pl.BlockSpec((tm, k), lambda i, j: (i, 0)),
            pl.BlockSpec((k, tn), lambda i, j: (0, j)),
            tab_spec, tab_spec, tab_spec,
        ],
        out_specs=pl.BlockSpec((tm, tn), lambda i, j: (i, j)),
        compiler_params=pltpu.CompilerParams(
            dimension_semantics=("parallel", "parallel"),
            vmem_limit_bytes=V7X_VMEM_LIMIT_BYTES),
        name="proj",
    )(x2d, w, *rope_tabs)


def _sb_logits_stage(q, k, strict):
    z = _nt_dot(q, k)
    sp = jnp.maximum(z, jnp.log2(1.0 + jnp.exp2(jnp.minimum(z, EXP2_ARG_MAX))))
    ls = z - sp
    if strict is not None:
        sp = jnp.where(strict, sp, 0.0)
    return ls, sp.astype(BF16), sp[:, :HEAD_DIM]


def _sb_weights_stage(ls, spb, sp0, v, u, acc_ref, carry_ref, rows, strict):
    groups = ls.shape[1] // HEAD_DIM
    excl = jnp.dot(spb, u, preferred_element_type=F32)
    carry = carry_ref[rows, :]
    a = jnp.exp2(ls - jnp.concatenate([carry] * groups, axis=1) - excl)
    if strict is not None:
        a = jnp.where(strict, a, 0.0)
    acc_ref[rows, :] += jnp.dot(a.astype(BF16), v, preferred_element_type=F32)
    carry_ref[rows, :] = carry + (excl[:, :1] + sp0[:, :1])


def _sb_kernel(q_ref, k_ref, v_ref, u_ref, o_ref, acc_ref, carry_ref, ls_ref, spb_ref, sp0_ref,
               *, blk, chunks):
    assert chunks % 2 == 0
    qi = pl.program_id(2)
    bq = blk * chunks
    u = u_ref[...]
    base = qi * chunks

    def k_tile(t):
        return k_ref[pl.ds(pl.multiple_of(t * blk, blk), blk), :]

    def v_tile(t):
        return v_ref[pl.ds(pl.multiple_of(t * blk, blk), blk), :]

    def tile_rows(t, r0, r1, diagonal, keep=None):
        rows = slice(r0, r1)
        strict = None
        if diagonal:
            row = lax.broadcasted_iota(jnp.int32, (r1 - r0, blk), 0)
            col = lax.broadcasted_iota(jnp.int32, (r1 - r0, blk), 1)
            strict = col < row
        if keep is not None:
            assert strict is None
            strict = jnp.broadcast_to(keep, (r1 - r0, blk))
        ls, spb, sp0 = _sb_logits_stage(q_ref[rows, :], k_tile(t), strict)
        _sb_weights_stage(ls, spb, sp0, v_tile(t), u, acc_ref, carry_ref, rows, strict)

    def min_carry(r0, r1):
        return jnp.min(carry_ref[r0:r1, :], axis=0)[0]

    acc_ref[...] = jnp.zeros(acc_ref.shape, F32)
    carry_ref[...] = jnp.zeros(carry_ref.shape, F32)

    for t in reversed(range(chunks)):
        tile_rows(base + t, t * blk, min((t + 2) * blk, bq), True)
    tile_rows(jnp.maximum(base - 1, 0), 0, blk, False, keep=base > 0)
    top = min_carry(0, blk)
    rest = min_carry(blk, bq)

    def rest_of_band():
        for t in reversed(range(chunks - 2)):
            tile_rows(base + t, (t + 2) * blk, bq, False)

        @pl.when(base > 0)
        def _():
            tile_rows(base - 1, blk, bq, False)

        return min_carry(blk, bq)

    rest = lax.cond(rest < SB_DEAD_CARRY_LOG2, rest_of_band, lambda: rest)

    def logits_to_slot(t, slot):
        ls_ref[slot], spb_ref[slot], sp0_ref[slot] = _sb_logits_stage(q_ref[...], k_tile(t), None)

    def position(p, par):
        _sb_weights_stage(ls_ref[par], spb_ref[par], sp0_ref[par], v_tile(base - 1 - p), u,
                          acc_ref, carry_ref, slice(None), None)
        logits_to_slot(jnp.maximum(base - 2 - p, 0), 1 - par)

    @pl.when(jnp.logical_and(base > 0, jnp.minimum(top, rest) < SB_DEAD_CARRY_LOG2))
    def _():
        logits_to_slot(base - 2, 1)
        for p in range(1, chunks):
            position(p, p % 2)

        def cond(state):
            g, go = state
            return jnp.logical_and(g < qi, go)

        def alive():
            return min_carry(0, bq) < SB_DEAD_CARRY_LOG2

        def body(state):
            g, _ = state
            for j in range(chunks):
                position(g * chunks + j, j % 2)
            return g + 1, alive()

        lax.while_loop(cond, body, (jnp.int32(1), alive()))

    o_ref[...] = acc_ref[...].astype(o_ref.dtype)


def _stick_breaking_attention(qkv, batch, seq, heads, *, blk=256, chunks=8):
    m = qkv.shape[0]
    bq = blk * chunks
    nq = seq // bq
    idx = jnp.arange(blk)
    u = (idx[:, None] > idx[None, :]).astype(BF16)
    return pl.pallas_call(
        functools.partial(_sb_kernel, blk=blk, chunks=chunks),
        out_shape=jax.ShapeDtypeStruct((m, heads * HEAD_DIM), BF16),
        grid=(batch, heads, nq),
        in_specs=[
            pl.BlockSpec((bq, HEAD_DIM), lambda b, h, i: (b * nq + i, h)),
            pl.BlockSpec((seq, HEAD_DIM), lambda b, h, i: (b, heads + h)),
            pl.BlockSpec((seq, HEAD_DIM), lambda b, h, i: (b, 2 * heads + h)),
            pl.BlockSpec((blk, blk), lambda b, h, i: (0, 0)),
        ],
        out_specs=pl.BlockSpec((bq, HEAD_DIM), lambda b, h, i: (b * nq + i, h)),
        scratch_shapes=[pltpu.VMEM((bq, HEAD_DIM), F32),
                        pltpu.VMEM((bq, HEAD_DIM), F32),
                        pltpu.VMEM((2, bq, blk), F32),
                        pltpu.VMEM((2, bq, blk), BF16),
                        pltpu.VMEM((2, bq, HEAD_DIM), F32)],
        compiler_params=pltpu.CompilerParams(
            dimension_semantics=("parallel", "parallel", "parallel"),
            vmem_limit_bytes=V7X_VMEM_LIMIT_BYTES),
        name="sb_attn",
    )(qkv, qkv, qkv, u)


def _diff_kernel(q_ref, k_ref, v_ref, lam_ref, g_ref, o_ref, m_ref, l_ref, acc_ref, kmax_ref, *,
                 blk, chunks, lambda_init):
    qi = pl.program_id(2)
    bq = blk * chunks
    wide = 2 * blk
    assert chunks == 4

    def scores(c, r0, k0, width, masked):
        lanes = slice(c * HEAD_DIM, (c + 1) * HEAD_DIM)
        s = _nt_dot(q_ref[r0:, lanes], k_ref[pl.ds(k0, width), lanes])
        if masked:
            row = lax.broadcasted_iota(jnp.int32, s.shape, 0)
            col = lax.broadcasted_iota(jnp.int32, s.shape, 1)
            s = jnp.where(col <= row, s, MASKED_LOGIT)
        return [s[:, g * HEAD_DIM:(g + 1) * HEAD_DIM] for g in range(width // HEAD_DIM)]

    def sweep(tile_fn):
        odd = qi & 1

        @pl.when(odd == 1)
        def _():
            for j in range(2):
                tile_fn(0, j * wide, wide, False)

        def body(i, carry):
            for j in range(4):
                tile_fn(0, pl.multiple_of((2 * odd + 4 * i + j) * wide, wide), wide, False)
            return carry

        lax.fori_loop(0, lax.shift_right_logical(qi, jnp.int32(1)), body, 0)
        for t in range(chunks):
            tile_fn(t * blk, pl.multiple_of((qi * chunks + t) * blk, blk), blk, True)

    def max_tile(r0, k0, width, masked):
        for c in range(2):
            m_ref[c, r0:, :] = functools.reduce(
                jnp.maximum, scores(c, r0, k0, width, masked), m_ref[c, r0:, :])

    def pv_tile(r0, k0, width, masked):
        v = v_ref[pl.ds(k0, width), :]
        for c in range(2):
            m = m_ref[c, r0:, :]
            p = [jnp.exp2(s - m) for s in scores(c, r0, k0, width, masked)]
            l_ref[c, r0:, :] += functools.reduce(jnp.add, p)
            acc_ref[c, r0:, :] += jnp.dot(jnp.concatenate(p, axis=1).astype(BF16), v,
                                          preferred_element_type=F32)

    half_r = lax.broadcasted_iota(jnp.int32, (2 * HEAD_DIM, 2 * HEAD_DIM), 0) >= HEAD_DIM
    half_c = lax.broadcasted_iota(jnp.int32, (2 * HEAD_DIM, 2 * HEAD_DIM), 1) >= HEAD_DIM
    ones_bd = jnp.where(half_r == half_c, 1.0, 0.0).astype(BF16)

    def sq_norms(x):
        return jnp.dot(x * x, ones_bd, preferred_element_type=F32)

    @pl.when(qi == 0)
    def _():
        def body(i, best):
            n = sq_norms(k_ref[pl.ds(pl.multiple_of(i * wide, wide), wide), :])
            return jnp.maximum(best, jnp.max(n.reshape(wide // 8, 8, 2 * HEAD_DIM), axis=0))

        best = lax.fori_loop(0, k_ref.shape[0] // wide, body, jnp.zeros(kmax_ref.shape, F32))
        kmax_ref[...] = jnp.broadcast_to(jnp.max(best, axis=0, keepdims=True), kmax_ref.shape)

    bound = jnp.sqrt(sq_norms(q_ref[...]) * kmax_ref[:1, :])
    for c in range(2):
        m_ref[c] = bound[:, c * HEAD_DIM:(c + 1) * HEAD_DIM]
    bound_ok = 2.0 * jnp.max(bound) <= SHIFT_SLACK_LOG2

    @pl.when(jnp.logical_not(bound_ok))
    def _():
        m_ref[...] = jnp.full(m_ref.shape, MASKED_LOGIT, F32)
        sweep(max_tile)
        for c in range(2):
            m_ref[c] = jnp.broadcast_to(jnp.max(m_ref[c], axis=1, keepdims=True),
                                        m_ref.shape[1:])

    l_ref[...] = jnp.zeros(l_ref.shape, F32)
    acc_ref[...] = jnp.zeros(acc_ref.shape, F32)
    sweep(pv_tile)

    lam = lam_ref[...]
    lam_full = (jnp.exp(jnp.sum(lam[0:1] * lam[1:2], axis=1, keepdims=True))
                - jnp.exp(jnp.sum(lam[2:3] * lam[3:4], axis=1, keepdims=True)) + lambda_init)
    inv_l = [1.0 / jnp.sum(l_ref[c], axis=1, keepdims=True) for c in range(2)]
    o = acc_ref[0] * inv_l[0] - lam_full * (acc_ref[1] * inv_l[1])
    o = o * lax.rsqrt(jnp.mean(o * o, axis=-1, keepdims=True) + SUBLN_EPS)
    o_ref[...] = (o * (g_ref[...] * (1.0 - lambda_init))).astype(o_ref.dtype)


def _differential_attention(qkv, lam, subln_g, batch, seq, heads, lambda_init, *, blk=256,
                            chunks=4):
    m = qkv.shape[0]
    bq = blk * chunks
    nq = seq // bq
    w = DIFF_V_DIM
    return pl.pallas_call(
        functools.partial(_diff_kernel, blk=blk, chunks=chunks, lambda_init=lambda_init),
        out_shape=jax.ShapeDtypeStruct((m, heads * w), BF16),
        grid=(batch, heads, nq),
        in_specs=[
            pl.BlockSpec((bq, w), lambda b, h, i: (b * nq + i, h)),
            pl.BlockSpec((seq, w), lambda b, h, i: (b, heads + h)),
            pl.BlockSpec((seq, w), lambda b, h, i: (b, 2 * heads + h)),
            pl.BlockSpec((4, HEAD_DIM), lambda b, h, i: (0, 0)),
            pl.BlockSpec((1, w), lambda b, h, i: (0, 0)),
        ],
        out_specs=pl.BlockSpec((bq, w), lambda b, h, i: (b * nq + i, h)),
        scratch_shapes=[pltpu.VMEM((2, bq, HEAD_DIM), F32), pltpu.VMEM((2, bq, HEAD_DIM), F32),
                        pltpu.VMEM((2, bq, w), F32), pltpu.VMEM((8, 2 * HEAD_DIM), F32)],
        compiler_params=pltpu.CompilerParams(
            dimension_semantics=("parallel", "parallel", "arbitrary"),
            vmem_limit_bytes=V7X_VMEM_LIMIT_BYTES),
        name="diff_attn",
    )(qkv, qkv, qkv, lam, subln_g.reshape(1, w))


def _out_ln_kernel(a_ref, w_ref, x_ref, g_ref, b_ref, of_ref, *, alpha):
    for r0 in range(0, of_ref.shape[0], EPILOGUE_ROWS):
        rows = slice(r0, r0 + EPILOGUE_ROWS)
        h = jnp.dot(a_ref[rows, :], w_ref[...], preferred_element_type=F32)
        of_ref[rows, :] = _layer_norm_rows(alpha * x_ref[rows, :] + h, g_ref[...], b_ref[...])


def _out_proj_ln(attn, w, x, g, b, alpha, *, tm=1024):
    m, k = attn.shape
    d = w.shape[1]
    row = pl.BlockSpec((tm, d), lambda i: (i, 0))
    vec = pl.BlockSpec((1, d), lambda i: (0, 0))
    return pl.pallas_call(
        functools.partial(_out_ln_kernel, alpha=alpha),
        out_shape=jax.ShapeDtypeStruct((m, d), F32),
        grid=(m // tm,),
        in_specs=[pl.BlockSpec((tm, k), lambda i: (i, 0)),
                  pl.BlockSpec((k, d), lambda i: (0, 0), pipeline_mode=pl.Buffered(1)),
                  row, vec, vec],
        out_specs=row,
        compiler_params=pltpu.CompilerParams(
            dimension_semantics=("parallel",),
            vmem_limit_bytes=V7X_VMEM_LIMIT_BYTES),
        name="out_proj_ln",
    )(attn, w, x, g.reshape(1, d), b.reshape(1, d))


def _mlp_ln_kernel(wu_ref, wd_ref, x_ref, g_ref, b_ref, of_ref, ob_ref, acc_ref, *, alpha):
    j = pl.program_id(1)

    @pl.when(j == 0)
    def _():
        acc_ref[...] = jnp.zeros_like(acc_ref)

    h = jnp.maximum(jnp.dot(x_ref[...].astype(BF16), wu_ref[...], preferred_element_type=F32),
                    0.0)
    acc_ref[...] += jnp.dot((h * h).astype(BF16), wd_ref[...], preferred_element_type=F32)

    @pl.when(j == pl.num_programs(1) - 1)
    def _():
        y = _layer_norm_rows(alpha * x_ref[...] + acc_ref[...], g_ref[...], b_ref[...])
        of_ref[...] = y
        ob_ref[...] = y.astype(BF16)


MLP_FF_TILE = 1024


def _mlp_ln(x, w_up_all, w_down_all, layer, g, b, alpha, *, tm=512, tf=MLP_FF_TILE):
    m, d = x.shape
    f = w_up_all.shape[2]
    row = lambda: pl.BlockSpec((tm, d), lambda i, j: (i, 0))
    vec = pl.BlockSpec((1, d), lambda i, j: (0, 0))
    return pl.pallas_call(
        functools.partial(_mlp_ln_kernel, alpha=alpha),
        out_shape=(jax.ShapeDtypeStruct((m, d), F32), jax.ShapeDtypeStruct((m, d), BF16)),
        grid=(m // tm, f // tf),
        in_specs=[pl.BlockSpec((None, d, tf), lambda i, j: (layer, 0, j)),
                  pl.BlockSpec((None, tf, d), lambda i, j: (layer, j, 0)),
                  row(), vec, vec],
        out_specs=(row(), row()),
        scratch_shapes=[pltpu.VMEM((tm, d), F32)],
        compiler_params=pltpu.CompilerParams(
            dimension_semantics=("parallel", "arbitrary"),
            vmem_limit_bytes=V7X_VMEM_LIMIT_BYTES),
        name="mlp_ln",
    )(w_up_all, w_down_all, x, g.reshape(1, d), b.reshape(1, d))


def _rope_tables(seq):
    half = ROT_DIM // 2
    inv_freq = ROPE_THETA ** (-jnp.arange(0, ROT_DIM, 2, dtype=F32) / ROT_DIM)
    ang = jnp.arange(seq, dtype=F32)[:, None] * inv_freq[None, :]
    cos, sin = jnp.cos(ang), jnp.sin(ang)
    zeros = lambda n: jnp.zeros((seq, n), F32)
    c = jnp.concatenate([cos, cos, jnp.ones((seq, HEAD_DIM - ROT_DIM), F32)], axis=1)
    sa = jnp.concatenate([-sin, zeros(HEAD_DIM - half)], axis=1)
    sb = jnp.concatenate([zeros(half), sin, zeros(HEAD_DIM - ROT_DIM)], axis=1)
    return c, sa, sb


def kernel(x, ln_g, ln_b, sb_w_qkv, sb_w_o, kv_w, diff_w_q, diff_lambda, diff_subln_g,
           diff_w_o, mlp_w_up, mlp_w_down):
    batch, seq, d_model = x.shape
    depth = ln_g.shape[0]
    n_a = sb_w_qkv.shape[0]
    assert depth - n_a == 1
    alpha = (2 * depth) ** 0.25
    sb_heads = d_model // HEAD_DIM
    diff_heads = d_model // DIFF_V_DIM
    rope_tabs = _rope_tables(seq)

    xf = x.reshape(batch * seq, d_model)
    xb = xf
    kv_wb = kv_w.astype(BF16)
    w_up_b = mlp_w_up.astype(BF16)
    w_down_b = mlp_w_down.astype(BF16)
    for layer in range(depth):
        if layer < n_a:
            qkv = _project(xb, sb_w_qkv[layer].astype(BF16), rope_tabs, 0, d_model, seq)
            attn = _stick_breaking_attention(qkv, batch, seq, sb_heads)
            w_o = sb_w_o[layer]
        else:
            j = layer - n_a
            w_cat = jnp.concatenate([diff_w_q[j].astype(BF16), kv_wb], axis=1)
            qkv = _project(xb, w_cat, rope_tabs, 2 * d_model, d_model, seq)
            lambda_init = 0.8 - 0.6 * math.exp(-0.3 * layer)
            attn = _differential_attention(qkv, diff_lambda[j], diff_subln_g[j], batch, seq,
                                           diff_heads, lambda_init)
            w_o = diff_w_o[j]
        xf = _out_proj_ln(attn, w_o.astype(BF16), xf, ln_g[layer, 0], ln_b[layer, 0], alpha)
        xf, xb = _mlp_ln(xf, w_up_b, w_down_b, layer, ln_g[layer, 1], ln_b[layer, 1], alpha)
    return xf.reshape(batch, seq, d_model)
```
